```python
import math
import jax, jax.numpy as jnp
from jax import lax
import numpy as np

D_MODEL = 1024
BATCH = 32
SEQ = 2048
DEPTH = 2

CONV_CH = D_MODEL // 2
CONV_WIDTH = 31
SB_HEADS = 8
SB_HEAD_DIM = 64
SB_WIDTH = SB_HEADS * SB_HEAD_DIM
Q_BLOCK = 128
EV_IN = 2 * CONV_CH + 3 * SB_WIDTH
EV_OUT_IN = CONV_CH + SB_WIDTH
LRU_WIDTH = D_MODEL
LRU_BLOCKS = 8
LRU_BLOCK_SIZE = LRU_WIDTH // LRU_BLOCKS
LRU_CONV_WIDTH = 4
LRU_C = 8.0
N_GROUPS = 4
EXPERTS_PER_GROUP = 4
N_EXPERTS = N_GROUPS * EXPERTS_PER_GROUP
TOP_K = 2
GROUP_SCORE_K = 2
D_FF_EXPERT = 512
MOE_BLOCK = 128

N_EVEN = (DEPTH + 1) // 2
N_ODD = DEPTH // 2
EPS = 1e-6

kernel_name = 'hybrid_conformer_stickbreak_rglru_moe'


def rms_norm(x, g):
    xf = x.astype(jnp.float32)
    y = xf * lax.rsqrt(jnp.mean(xf * xf, axis=-1, keepdims=True) + EPS)
    return (y * g.astype(jnp.float32)).astype(x.dtype)


def layer_norm(x, g, b):
    xf = x.astype(jnp.float32)
    mu = jnp.mean(xf, axis=-1, keepdims=True)
    xc = xf - mu
    var = jnp.mean(xc * xc, axis=-1, keepdims=True)
    y = xc * lax.rsqrt(var + EPS) * g.astype(jnp.float32) + b.astype(jnp.float32)
    return y.astype(x.dtype)


def causal_dwconv(x, w, b):
    width = w.shape[0]
    y = lax.conv_general_dilated(
        x, w[:, None, :].astype(x.dtype), window_strides=(1,),
        padding=[(width - 1, 0)], dimension_numbers=('NWC', 'WIO', 'NWC'),
        feature_group_count=x.shape[-1])
    return y + b.astype(x.dtype)


def stick_breaking_attention(q, k, v):
    seq = q.shape[2]
    scale = 1.0 / math.sqrt(q.shape[-1])
    outs = []
    for blk in range(seq // Q_BLOCK):
        q0 = blk * Q_BLOCK
        end = q0 + Q_BLOCK
        qb = q[:, :, q0:end]
        kb = k[:, :, :end]
        vb = v[:, :, :end]
        z = jnp.einsum('bhqd,bhkd->bhqk', qb, kb).astype(jnp.float32) * scale
        qpos = q0 + jnp.arange(Q_BLOCK)[:, None]
        kpos = jnp.arange(end)[None, :]
        causal = kpos < qpos
        log_1m = jnp.where(causal, jax.nn.log_sigmoid(-z), 0.0)
        between = lax.cumsum(log_1m, axis=3, reverse=True) - log_1m
        wts = jnp.where(causal, jnp.exp(jax.nn.log_sigmoid(z) + between), 0.0)
        outs.append(jnp.einsum('bhqk,bhkd->bhqd', wts.astype(v.dtype), vb))
    return jnp.concatenate(outs, axis=2)


def even_mixer(h, in_w, dw_w, dw_b, ln_g, ln_b, q_g, k_g, out_w):
    bsz, seq, _ = h.shape
    proj = h @ in_w
    a_val, a_gate, q, k, v = jnp.split(
        proj, [CONV_CH, 2 * CONV_CH, 2 * CONV_CH + SB_WIDTH, 2 * CONV_CH + 2 * SB_WIDTH], axis=-1)
    u = a_val * jax.nn.sigmoid(a_gate)
    u = causal_dwconv(u, dw_w, dw_b)
    u = jax.nn.silu(layer_norm(u, ln_g, ln_b))
    def heads(t):
        return t.reshape(bsz, seq, SB_HEADS, SB_HEAD_DIM)
    qh = rms_norm(heads(q), q_g).transpose(0, 2, 1, 3)
    kh = rms_norm(heads(k), k_g).transpose(0, 2, 1, 3)
    vh = heads(v).transpose(0, 2, 1, 3)
    o = stick_breaking_attention(qh, kh, vh).transpose(0, 2, 1, 3).reshape(bsz, seq, SB_WIDTH)
    return jnp.concatenate([u, o], axis=-1) @ out_w


def block_diag_linear(x, w, b):
    nb, bs, _ = w.shape
    xb = x.reshape(x.shape[:-1] + (nb, bs))
    return jnp.einsum('bsnk,nkj->bsnj', xb, w).reshape(x.shape) + b


def lru_combine(earlier, later):
    a1, b1 = earlier
    a2, b2 = later
    return a1 * a2, a2 * b1 + b2


def odd_mixer(h, in_w, conv_w, conv_b, rg_w, rg_b, ig_w, ig_b, lam, out_w):
    proj = h @ in_w
    y_branch, x_branch = jnp.split(proj, 2, axis=-1)
    xc = causal_dwconv(x_branch, conv_w, conv_b)
    r = jax.nn.sigmoid(block_diag_linear(xc, rg_w, rg_b))
    i = jax.nn.sigmoid(block_diag_linear(xc, ig_w, ig_b))
    log_a = LRU_C * r.astype(jnp.float32) * jax.nn.log_sigmoid(lam.astype(jnp.float32))
    a = jnp.exp(log_a)
    mult = jnp.sqrt(-jnp.expm1(2.0 * log_a))
    b_in = mult * (i * xc).astype(jnp.float32)
    _, hseq = lax.associative_scan(lru_combine, (a, b_in), axis=1)
    return (jax.nn.gelu(y_branch) * hseq.astype(h.dtype)) @ out_w


def moe(h2, router_w, router_b, w1, w3, w2):
    n_tok, d = h2.shape
    logits = h2.astype(jnp.float32) @ router_w.astype(jnp.float32)
    scores = jax.nn.sigmoid(logits)
    biased = (scores + router_b.astype(jnp.float32)).reshape(n_tok, N_GROUPS, EXPERTS_PER_GROUP)
    group_score = jnp.sum(lax.top_k(biased, GROUP_SCORE_K)[0], axis=-1)
    gidx = jnp.argmax(group_score, axis=-1)
    in_group = jnp.take_along_axis(biased, gidx[:, None, None], axis=1)[:, 0]
    _, loc = lax.top_k(in_group, TOP_K)
    eid = gidx[:, None] * EXPERTS_PER_GROUP + loc
    sel = jnp.take_along_axis(scores, eid, axis=1)
    gates = (sel / jnp.sum(sel, axis=-1, keepdims=True)).astype(h2.dtype)
    n_asg = n_tok * TOP_K
    flat_e = eid.reshape(n_asg)
    flat_tok = jnp.repeat(jnp.arange(n_tok, dtype=jnp.int32), TOP_K)
    flat_g = gates.reshape(n_asg)
    order = jnp.argsort(flat_e)
    se = flat_e[order]
    stok = flat_tok[order]
    sg = flat_g[order]
    counts = jnp.bincount(flat_e, length=N_EXPERTS)
    starts = jnp.cumsum(counts) - counts
    padded = (counts + MOE_BLOCK - 1) // MOE_BLOCK * MOE_BLOCK
    pad_end = jnp.cumsum(padded)
    pad_start = pad_end - padded
    dest = pad_start[se] + (jnp.arange(n_asg) - starts[se])
    n_blocks = -(-n_asg // MOE_BLOCK) + N_EXPERTS
    buf_len = n_blocks * MOE_BLOCK
    buf_tok = jnp.zeros((buf_len,), jnp.int32).at[dest].set(stok)
    buf_g = jnp.zeros((buf_len,), h2.dtype).at[dest].set(sg)
    blk_e = jnp.minimum(
        jnp.searchsorted(pad_end, jnp.arange(n_blocks) * MOE_BLOCK, side='right'), N_EXPERTS - 1)

    def run_block(args):
        tok, g, e = args
        xb = h2[tok]
        y = (jax.nn.silu(xb @ w1[e]) * (xb @ w3[e])) @ w2[e]
        return y * g[:, None]

    ys = lax.map(run_block, (buf_tok.reshape(n_blocks, MOE_BLOCK),
                             buf_g.reshape(n_blocks, MOE_BLOCK), blk_e))
    return jnp.zeros_like(h2).at[buf_tok].add(ys.reshape(buf_len, d))


def setup_inputs(seed: int = 0) -> dict:
    key = jax.random.key(seed)
    ks = jax.random.split(key, 28)
    f32 = jnp.float32

    def nrm(k, shape, scale):
        return jax.random.normal(k, shape, f32) * scale

    def gain(k, shape):
        return 1.0 + 0.05 * jax.random.normal(k, shape, f32)

    u = jax.random.uniform(ks[24], (N_ODD, LRU_WIDTH), f32, 0.9, 0.999)
    s = u ** (1.0 / LRU_C)
    lam = jnp.log(s) - jnp.log1p(-s)
    return {
        'x': nrm(ks[0], (BATCH, SEQ, D_MODEL), 1.0),
        'c': nrm(ks[1], (BATCH, D_MODEL), 1.0),
        'mod_w': nrm(ks[2], (DEPTH, D_MODEL, 6 * D_MODEL), 0.5 * D_MODEL ** -0.5),
        'mod_b': nrm(ks[3], (DEPTH, 6 * D_MODEL), 0.02),
        'mix_norm_g': gain(ks[4], (DEPTH, D_MODEL)),
        'ffn_norm_g': gain(ks[5], (DEPTH, D_MODEL)),
        'ev_in_w': nrm(ks[6], (N_EVEN, D_MODEL, EV_IN), D_MODEL ** -0.5),
        'ev_dw_w': nrm(ks[7], (N_EVEN, CONV_WIDTH, CONV_CH), CONV_WIDTH ** -0.5),
        'ev_dw_b': nrm(ks[8], (N_EVEN, CONV_CH), 0.02),
        'ev_ln_g': gain(ks[9], (N_EVEN, CONV_CH)),
        'ev_ln_b': nrm(ks[10], (N_EVEN, CONV_CH), 0.02),
        'ev_q_g': gain(ks[11], (N_EVEN, SB_HEAD_DIM)),
        'ev_k_g': gain(ks[12], (N_EVEN, SB_HEAD_DIM)),
        'ev_out_w': nrm(ks[13], (N_EVEN, EV_OUT_IN, D_MODEL), EV_OUT_IN ** -0.5),
        'od_in_w': nrm(ks[14], (N_ODD, D_MODEL, 2 * LRU_WIDTH), D_MODEL ** -0.5),
        'od_conv_w': nrm(ks[15], (N_ODD, LRU_CONV_WIDTH, LRU_WIDTH), LRU_CONV_WIDTH ** -0.5),
        'od_conv_b': nrm(ks[16], (N_ODD, LRU_WIDTH), 0.02),
        'od_rg_w': nrm(ks[17], (N_ODD, LRU_BLOCKS, LRU_BLOCK_SIZE, LRU_BLOCK_SIZE), LRU_BLOCK_SIZE ** -0.5),
        'od_rg_b': nrm(ks[18], (N_ODD, LRU_WIDTH), 0.02),
        'od_ig_w': nrm(ks[19], (N_ODD, LRU_BLOCKS, LRU_BLOCK_SIZE, LRU_BLOCK_SIZE), LRU_BLOCK_SIZE ** -0.5),
        'od_ig_b': nrm(ks[20], (N_ODD, LRU_WIDTH), 0.02),
        'od_lam': lam,
        'od_out_w': nrm(ks[21], (N_ODD, LRU_WIDTH, D_MODEL), LRU_WIDTH ** -0.5),
        'router_w': nrm(ks[22], (D_MODEL, N_EXPERTS), D_MODEL ** -0.5),
        'router_b': nrm(ks[23], (N_EXPERTS,), 0.01),
        'ex_w1': nrm(ks[25], (DEPTH, N_EXPERTS, D_MODEL, D_FF_EXPERT), D_MODEL ** -0.5),
        'ex_w3': nrm(ks[26], (DEPTH, N_EXPERTS, D_MODEL, D_FF_EXPERT), D_MODEL ** -0.5),
        'ex_w2': nrm(ks[27], (DEPTH, N_EXPERTS, D_FF_EXPERT, D_MODEL), D_FF_EXPERT ** -0.5),
    }


def reference(x, c, mod_w, mod_b, mix_norm_g, ffn_norm_g, ev_in_w, ev_dw_w, ev_dw_b,
              ev_ln_g, ev_ln_b, ev_q_g, ev_k_g, ev_out_w, od_in_w, od_conv_w, od_conv_b,
              od_rg_w, od_rg_b, od_ig_w, od_ig_b, od_lam, od_out_w, router_w, router_b,
              ex_w1, ex_w3, ex_w2):
    bsz, seq, d = x.shape
    cond = jax.nn.silu(c)
    for layer in range(DEPTH):
        mod = cond @ mod_w[layer] + mod_b[layer]
        sh1, sc1, g1, sh2, sc2, g2 = [m[:, None, :] for m in jnp.split(mod, 6, axis=-1)]
        h = rms_norm(x, mix_norm_g[layer]) * (1 + sc1) + sh1
        if layer % 2 == 0:
            e = layer // 2
            mix = even_mixer(h, ev_in_w[e], ev_dw_w[e], ev_dw_b[e], ev_ln_g[e], ev_ln_b[e],
                             ev_q_g[e], ev_k_g[e], ev_out_w[e])
        else:
            o = layer // 2
            mix = odd_mixer(h, od_in_w[o], od_conv_w[o], od_conv_b[o], od_rg_w[o], od_rg_b[o],
                            od_ig_w[o], od_ig_b[o], od_lam[o], od_out_w[o])
        x = x + g1 * mix
        h = rms_norm(x, ffn_norm_g[layer]) * (1 + sc2) + sh2
        ffn = moe(h.reshape(bsz * seq, d), router_w, router_b,
                  ex_w1[layer], ex_w3[layer], ex_w2[layer]).reshape(bsz, seq, d)
        x = x + g2 * ffn
    return x
```

```python
import functools
import math

import jax
import jax.numpy as jnp
from jax import lax
from jax.experimental import pallas as pl
from jax.experimental.pallas import tpu as pltpu

F32 = jnp.float32
BF16 = jnp.bfloat16
EPS = 1e-6
LRU_C = 8.0
N_GROUPS = 4
EXPERTS_PER_GROUP = 4
PAIRS_PER_GROUP = 6
N_CLASSES = N_GROUPS * PAIRS_PER_GROUP
_PAIR_LO = (0, 0, 0, 1, 1, 2)
_PAIR_HI = (1, 2, 3, 2, 3, 3)

V7X_VMEM_LIMIT_BYTES = 56 * 1024 * 1024
ROW_TILE = 512
SEQ_TILE = 256
ATT_TILE = 128
MOE_TILE = 256
PERM_TILE = 512


def _cparams(*sem):
    return pltpu.CompilerParams(dimension_semantics=sem, vmem_limit_bytes=V7X_VMEM_LIMIT_BYTES)


def _dot(a, b):
    return jnp.dot(a, b, preferred_element_type=F32)


def _dot_nt(a, b):
    return lax.dot_general(a, b, (((1,), (1,)), ((), ())), preferred_element_type=F32)


def _split_bf16(x):
    hi = x.astype(BF16)
    lo = (x - hi.astype(F32)).astype(BF16)
    return hi, lo


def _neg_softplus(z):
    return -(jnp.maximum(z, 0.0) + jnp.log1p(jnp.exp(-jnp.abs(z))))


def _norm_mod(x, g, shift, scale):
    ms = jnp.mean(x * x, axis=-1, keepdims=True)
    y = x * lax.rsqrt(ms + EPS) * g
    return y * (1.0 + scale) + shift


def _mod_kernel(c_ref, w_ref, b_ref, o_ref):
    c = c_ref[...]
    cond = (c * jax.nn.sigmoid(c)).astype(BF16)
    o_ref[...] = _dot(cond, w_ref[...].astype(BF16)) + b_ref[...]


def _modulation(c, mod_w, mod_b):
    depth, d, n = mod_w.shape
    bsz = c.shape[0]
    tn = min(n, 1536)
    out = pl.pallas_call(
        _mod_kernel,
        grid=(depth, n // tn),
        in_specs=[pl.BlockSpec((bsz, d), lambda l, j: (0, 0)),
                  pl.BlockSpec((None, d, tn), lambda l, j: (l, 0, j)),
                  pl.BlockSpec((None, 1, tn), lambda l, j: (l, 0, j))],
        out_specs=pl.BlockSpec((None, bsz, tn), lambda l, j: (l, 0, j)),
        out_shape=jax.ShapeDtypeStruct((depth, bsz, n), F32),
        compiler_params=_cparams("arbitrary", "arbitrary"),
        name="adaln_mod",
    )(c, mod_w, mod_b.reshape(depth, 1, n))
    return out.reshape(depth, bsz, 6, d)


def _inproj_even_kernel(x_ref, g_ref, mod_ref, w_ref, e_ref, qg_ref, kg_ref,
                        a_ref, q_ref, k_ref, v_ref, *, conv2, sbw, head_dim):
    h = _norm_mod(x_ref[...], g_ref[...], mod_ref[0:1, :], mod_ref[1:2, :]).astype(BF16)
    a_ref[...] = _dot(h, w_ref[:, 0:conv2])

    def head_rms(t, gain):
        hi, lo = _split_bf16(t * t)
        ss = _dot(hi, e_ref[...]) + _dot(lo, e_ref[...])
        return t * lax.rsqrt(ss * (1.0 / head_dim) + EPS) * gain

    q = _dot(h, w_ref[:, conv2:conv2 + sbw])
    q_ref[...] = (head_rms(q, qg_ref[...]) * (1.0 / math.sqrt(head_dim))).astype(BF16)
    k = _dot(h, w_ref[:, conv2 + sbw:conv2 + 2 * sbw])
    k_ref[...] = head_rms(k, kg_ref[...]).astype(BF16)
    v_ref[...] = _dot(h, w_ref[:, conv2 + 2 * sbw:conv2 + 3 * sbw]).astype(BF16)


def _inproj_even(x, norm_g, mod, in_w, q_g, k_g, conv_ch):
    bsz, seq, d = x.shape
    n = in_w.shape[1]
    conv2 = 2 * conv_ch
    sbw = (n - conv2) // 3
    head_dim = q_g.shape[0]
    heads = sbw // head_dim
    tm = min(ROW_TILE, seq)
    head_sum = jnp.kron(jnp.eye(heads, dtype=F32), jnp.ones((head_dim, head_dim), F32)).astype(BF16)
    kern = functools.partial(_inproj_even_kernel, conv2=conv2, sbw=sbw, head_dim=head_dim)
    row = lambda b, i: (b, i, 0)
    const = lambda b, i: (0, 0)
    return pl.pallas_call(
        kern,
        grid=(bsz, seq // tm),
        in_specs=[pl.BlockSpec((None, tm, d), row),
                  pl.BlockSpec((1, d), const),
                  pl.BlockSpec((None, 6, d), lambda b, i: (b, 0, 0)),
                  pl.BlockSpec((d, n), const),
                  pl.BlockSpec((sbw, sbw), const),
                  pl.BlockSpec((1, sbw), const),
                  pl.BlockSpec((1, sbw), const)],
        out_specs=[pl.BlockSpec((None, tm, conv2), row),
                   pl.BlockSpec((None, tm, sbw), row),
                   pl.BlockSpec((None, tm, sbw), row),
                   pl.BlockSpec((None, tm, sbw), row)],
        out_shape=[jax.ShapeDtypeStruct((bsz, seq, conv2), F32),
                   jax.ShapeDtypeStruct((bsz, seq, sbw), BF16),
                   jax.ShapeDtypeStruct((bsz, seq, sbw), BF16),
                   jax.ShapeDtypeStruct((bsz, seq, sbw), BF16)],
        compiler_params=_cparams("arbitrary", "arbitrary"),
        name="even_in_proj",
    )(x, norm_g.reshape(1, d), mod, in_w.astype(BF16), head_sum,
      jnp.tile(q_g, heads).reshape(1, sbw), jnp.tile(k_g, heads).reshape(1, sbw))


def _conv_module_kernel(a_ref, w_ref, b_ref, lg_ref, lb_ref, o_ref, buf_ref, *, ch, width, halo, ts, chunk):
    si = pl.program_id(1)

    @pl.when(si == 0)
    def _():
        buf_ref[0:halo, :] = jnp.zeros((halo, ch), F32)

    @pl.when(si > 0)
    def _():
        buf_ref[0:halo, :] = buf_ref[ts:ts + halo, :]

    val = a_ref[:, 0:ch]
    gate = a_ref[:, ch:2 * ch]
    buf_ref[halo:halo + ts, :] = val * jax.nn.sigmoid(gate)

    off = halo - (width - 1)
    for c in range(ts // chunk):
        acc = jnp.broadcast_to(b_ref[...], (chunk, ch))
        for k in range(width):
            r0 = c * chunk + off + k
            acc = acc + w_ref[k:k + 1, :] * buf_ref[r0:r0 + chunk, :]
        mu = jnp.mean(acc, axis=-1, keepdims=True)
        xc = acc - mu
        var = jnp.mean(xc * xc, axis=-1, keepdims=True)
        y = xc * lax.rsqrt(var + EPS) * lg_ref[...] + lb_ref[...]
        o_ref[c * chunk:(c + 1) * chunk, :] = (y * jax.nn.sigmoid(y)).astype(BF16)


def _conv_module(a, dw_w, dw_b, ln_g, ln_b):
    bsz, seq, ch2 = a.shape
    ch = ch2 // 2
    width = dw_w.shape[0]
    halo = -(-(width - 1) // 8) * 8
    ts = min(SEQ_TILE, seq)
    chunk = min(32, ts)
    kern = functools.partial(_conv_module_kernel, ch=ch, width=width, halo=halo, ts=ts, chunk=chunk)
    const = lambda b, i: (0, 0)
    return pl.pallas_call(
        kern,
        grid=(bsz, seq // ts),
        in_specs=[pl.BlockSpec((None, ts, ch2), lambda b, i: (b, i, 0)),
                  pl.BlockSpec((width, ch), const),
                  pl.BlockSpec((1, ch), const),
                  pl.BlockSpec((1, ch), const),
                  pl.BlockSpec((1, ch), const)],
        out_specs=pl.BlockSpec((None, ts, ch), lambda b, i: (b, i, 0)),
        out_shape=jax.ShapeDtypeStruct((bsz, seq, ch), BF16),
        scratch_shapes=[pltpu.VMEM((halo + ts, ch), F32)],
        compiler_params=_cparams("arbitrary", "arbitrary"),
        name="conformer_conv",
    )(a, dw_w, dw_b.reshape(1, ch), ln_g.reshape(1, ch), ln_b.reshape(1, ch))


def _sb_attention_kernel(q_ref, k_ref, v_ref, u_ref, o_ref, *, head_dim, tq):
    qi = pl.program_id(2)
    heads_here = q_ref.shape[1] // head_dim
    rows = lax.broadcasted_iota(jnp.int32, (tq, tq), 0)
    cols = lax.broadcasted_iota(jnp.int32, (tq, tq), 1)
    causal = cols < rows

    def tile(q, kj, vj, run, acc, mask):
        z = _dot_nt(q, kj)
        l1m = _neg_softplus(z)
        if mask is not None:
            l1m = jnp.where(mask, l1m, 0.0)
        hi, lo = _split_bf16(l1m)
        s = _dot(hi, u_ref[...]) + _dot(lo, u_ref[...])
        w = jnp.exp(z + l1m + s[:, 0:tq] + run)
        if mask is not None:
            w = jnp.where(mask, w, 0.0)
        acc = acc + _dot(w.astype(BF16), vj)
        return run + s[:, tq:2 * tq], acc

    for hh in range(heads_here):
        lanes = slice(hh * head_dim, (hh + 1) * head_dim)
        q = q_ref[:, lanes]
        d0 = pl.multiple_of(qi * tq, tq)
        run = jnp.zeros((tq, tq), F32)
        acc = jnp.zeros((tq, head_dim), F32)
        run, acc = tile(q, k_ref[pl.ds(d0, tq), lanes], v_ref[pl.ds(d0, tq), lanes], run, acc, causal)

        def body(step, carry):
            run, acc = carry
            k0 = pl.multiple_of((qi - 1 - step) * tq, tq)
            return tile(q, k_ref[pl.ds(k0, tq), lanes], v_ref[pl.ds(k0, tq), lanes], run, acc, None)

        run, acc = lax.fori_loop(0, qi, body, (run, acc))
        o_ref[:, lanes] = acc.astype(BF16)


def _sb_attention(q, k, v, head_dim):
    bsz, seq, sbw = q.shape
    tq = min(ATT_TILE, seq)
    lane_blk = min(128, sbw)
    later = (lax.broadcasted_iota(jnp.int32, (tq, tq), 0) > lax.broadcasted_iota(jnp.int32, (tq, tq), 1))
    suffix = jnp.concatenate([later.astype(BF16), jnp.ones((tq, tq), BF16)], axis=1)
    kern = functools.partial(_sb_attention_kernel, head_dim=head_dim, tq=tq)
    return pl.pallas_call(
        kern,
        grid=(bsz, sbw // lane_blk, seq // tq),
        in_specs=[pl.BlockSpec((None, tq, lane_blk), lambda b, h, i: (b, i, h)),
                  pl.BlockSpec((None, seq, lane_blk), lambda b, h, i: (b, 0, h)),
                  pl.BlockSpec((None, seq, lane_blk), lambda b, h, i: (b, 0, h)),
                  pl.BlockSpec((tq, 2 * tq), lambda b, h, i: (0, 0))],
        out_specs=pl.BlockSpec((None, tq, lane_blk), lambda b, h, i: (b, i, h)),
        out_shape=jax.ShapeDtypeStruct((bsz, seq, sbw), BF16),
        compiler_params=_cparams("arbitrary", "arbitrary", "arbitrary"),
        name="stickbreak_attn",
    )(q, k, v, suffix)


def _outproj_kernel(*refs, n_in):
    x_ref, mod_ref = refs[0], refs[1]
    ins = refs[2:2 + n_in]
    ws = refs[2 + n_in:2 + 2 * n_in]
    o_ref = refs[2 + 2 * n_in]
    acc = _dot(ins[0][...], ws[0][...])
    for t_ref, w_ref in zip(ins[1:], ws[1:]):
        acc = acc + _dot(t_ref[...], w_ref[...])
    o_ref[...] = x_ref[...] + mod_ref[2:3, :] * acc


def _outproj_residual(x, mod, parts, weights):
    bsz, seq, d = x.shape
    tm = min(ROW_TILE, seq)
    row = lambda b, i: (b, i, 0)
    const = lambda b, i: (0, 0)
    in_specs = [pl.BlockSpec((None, tm, d), row), pl.BlockSpec((None, 6, d), lambda b, i: (b, 0, 0))]
    in_specs += [pl.BlockSpec((None, tm, p.shape[-1]), row) for p in parts]
    in_specs += [pl.BlockSpec(w.shape, const) for w in weights]
    return pl.pallas_call(
        functools.partial(_outproj_kernel, n_in=len(parts)),
        grid=(bsz, seq // tm),
        in_specs=in_specs,
        out_specs=pl.BlockSpec((None, tm, d), row),
        out_shape=jax.ShapeDtypeStruct((bsz, seq, d), F32),
        compiler_params=_cparams("arbitrary", "arbitrary"),
        name="out_proj_residual",
    )(x, mod, *parts, *[w.astype(BF16) for w in weights])


def _inproj_odd_kernel(x_ref, g_ref, mod_ref, w_ref, o_ref, *, n, tn):
    h = _norm_mod(x_ref[...], g_ref[...], mod_ref[0:1, :], mod_ref[1:2, :]).astype(BF16)
    for j in range(n // tn):
        o_ref[:, j * tn:(j + 1) * tn] = _dot(h, w_ref[:, j * tn:(j + 1) * tn])


def _inproj_odd(x, norm_g, mod, in_w):
    bsz, seq, d = x.shape
    n = in_w.shape[1]
    tm = min(ROW_TILE, seq)
    tn = min(512, n)
    row = lambda b, i: (b, i, 0)
    const = lambda b, i: (0, 0)
    return pl.pallas_call(
        functools.partial(_inproj_odd_kernel, n=n, tn=tn),
        grid=(bsz, seq // tm),
        in_specs=[pl.BlockSpec((None, tm, d), row),
                  pl.BlockSpec((1, d), const),
                  pl.BlockSpec((None, 6, d), lambda b, i: (b, 0, 0)),
                  pl.BlockSpec((d, n), const)],
        out_specs=pl.BlockSpec((None, tm, n), row),
        out_shape=jax.ShapeDtypeStruct((bsz, seq, n), F32),
        compiler_params=_cparams("arbitrary", "arbitrary"),
        name="odd_in_proj",
    )(x, norm_g.reshape(1, d), mod, in_w.astype(BF16))


def _rglru_kernel(y_ref, x_ref, cw_ref, cb_ref, gw_ref, rb_ref, ib_ref, lam_ref, o_ref,
                  xbuf_ref, hprev_ref, *, ts, width, bs, nb):
    si = pl.program_id(1)
    halo = 8
    lw = nb * bs

    @pl.when(si == 0)
    def _():
        xbuf_ref[0:halo, :] = jnp.zeros((halo, lw), F32)
        hprev_ref[...] = jnp.zeros((8, lw), F32)

    @pl.when(si > 0)
    def _():
        xbuf_ref[0:halo, :] = xbuf_ref[ts:ts + halo, :]

    xbuf_ref[halo:halo + ts, :] = x_ref[...]
    sub = lax.broadcasted_iota(jnp.int32, (8, bs), 0)
    off = halo - (width - 1)

    for nblk in range(nb):
        lanes = slice(nblk * bs, (nblk + 1) * bs)
        xc = jnp.broadcast_to(cb_ref[:, lanes], (ts, bs))
        for k in range(width):
            xc = xc + cw_ref[k:k + 1, lanes] * xbuf_ref[off + k:off + k + ts, lanes]
        gates = _dot(xc.astype(BF16), gw_ref[nblk])
        r = jax.nn.sigmoid(gates[:, 0:bs] + rb_ref[:, lanes])
        ig = jax.nn.sigmoid(gates[:, bs:2 * bs] + ib_ref[:, lanes])
        log_a = LRU_C * r * _neg_softplus(-lam_ref[:, lanes])
        a = jnp.exp(log_a)
        mult = jnp.sqrt(-jnp.tanh(log_a) * (a * a + 1.0))
        b_in = mult * (ig * xc)

        h_last = hprev_ref[0:1, lanes]
        yv = y_ref[:, lanes]
        for g in range(ts // 8):
            ag = a[g * 8:(g + 1) * 8, :]
            bg = b_in[g * 8:(g + 1) * 8, :]
            for d in (1, 2, 4):
                a_sh = jnp.where(sub >= d, pltpu.roll(ag, d, 0), 1.0)
                b_sh = jnp.where(sub >= d, pltpu.roll(bg, d, 0), 0.0)
                bg = ag * b_sh + bg
                ag = ag * a_sh
            hg = ag * h_last + bg
            h_last = hg[7:8, :]
            yg = yv[g * 8:(g + 1) * 8, :]
            gelu = 0.5 * yg * (1.0 + jnp.tanh(math.sqrt(2.0 / math.pi) * (yg + 0.044715 * (yg * yg * yg))))
            o_ref[g * 8:(g + 1) * 8, lanes] = (gelu * hg).astype(BF16)
        hprev_ref[0:1, lanes] = h_last


def _rglru(proj, conv_w, conv_b, rg_w, rg_b, ig_w, ig_b, lam):
    bsz, seq, n2 = proj.shape
    lw = n2 // 2
    nb, bs, _ = rg_w.shape
    width = conv_w.shape[0]
    ts = min(SEQ_TILE, seq)
    gate_w = jnp.concatenate([rg_w, ig_w], axis=-1).astype(BF16)
    kern = functools.partial(_rglru_kernel, ts=ts, width=width, bs=bs, nb=nb)
    const = lambda b, i: (0, 0)
    return pl.pallas_call(
        kern,
        grid=(bsz, seq // ts),
        in_specs=[pl.BlockSpec((None, ts, lw), lambda b, i: (b, i, 0)),
                  pl.BlockSpec((None, ts, lw), lambda b, i: (b, i, 1)),
                  pl.BlockSpec((width, lw), const),
                  pl.BlockSpec((1, lw), const),
                  pl.BlockSpec((nb, bs, 2 * bs), lambda b, i: (0, 0, 0)),
                  pl.BlockSpec((1, lw), const),
                  pl.BlockSpec((1, lw), const),
                  pl.BlockSpec((1, lw), const)],
        out_specs=pl.BlockSpec((None, ts, lw), lambda b, i: (b, i, 0)),
        out_shape=jax.ShapeDtypeStruct((bsz, seq, lw), BF16),
        scratch_shapes=[pltpu.VMEM((8 + ts, lw), F32), pltpu.VMEM((8, lw), F32)],
        compiler_params=_cparams("arbitrary", "arbitrary"),
        name="rglru",
    )(proj, proj, conv_w, conv_b.reshape(1, lw), gate_w, rg_b.reshape(1, lw), ig_b.reshape(1, lw),
      lam.reshape(1, lw))


def _router_kernel(x_ref, g_ref, mod_ref, rw_ref, h_ref, s_ref):
    h = _norm_mod(x_ref[...], g_ref[...], mod_ref[3:4, :], mod_ref[4:5, :])
    h_ref[...] = h
    hi, lo = _split_bf16(h)
    whi, wlo = _split_bf16(rw_ref[...])
    logits = _dot(hi, whi) + (_dot(lo, whi) + _dot(hi, wlo))
    s_ref[...] = jax.nn.sigmoid(logits)


def _router(x, norm_g, mod, router_w):
    bsz, seq, d = x.shape
    ne = router_w.shape[1]
    lanes = 128
    tm = min(ROW_TILE, seq)
    rw = jnp.zeros((d, lanes), F32).at[:, :ne].set(router_w)
    row = lambda b, i: (b, i, 0)
    const = lambda b, i: (0, 0)
    h, scores = pl.pallas_call(
        _router_kernel,
        grid=(bsz, seq // tm),
        in_specs=[pl.BlockSpec((None, tm, d), row),
                  pl.BlockSpec((1, d), const),
                  pl.BlockSpec((None, 6, d), lambda b, i: (b, 0, 0)),
                  pl.BlockSpec((d, lanes), const)],
        out_specs=[pl.BlockSpec((None, tm, d), row), pl.BlockSpec((None, tm, lanes), row)],
        out_shape=[jax.ShapeDtypeStruct((bsz, seq, d), F32), jax.ShapeDtypeStruct((bsz, seq, lanes), F32)],
        compiler_params=_cparams("arbitrary", "arbitrary"),
        name="moe_router",
    )(x, norm_g.reshape(1, d), mod, rw)
    return h.reshape(bsz * seq, d), scores.reshape(bsz * seq, lanes)[:, :ne]


def _routing_tables(scores, router_b, n_blocks):
    n_tok, ne = scores.shape
    biased = (scores + router_b.astype(F32)).reshape(n_tok, N_GROUPS, EXPERTS_PER_GROUP)
    group_score = jnp.sum(lax.top_k(biased, 2)[0], axis=-1)
    gidx = jnp.argmax(group_score, axis=-1).astype(jnp.int32)
    in_group = jnp.take_along_axis(biased, gidx[:, None, None], axis=1)[:, 0]
    _, loc = lax.top_k(in_group, 2)
    loc = loc.astype(jnp.int32)
    eid = gidx[:, None] * EXPERTS_PER_GROUP + loc
    sel = jnp.take_along_axis(scores, eid, axis=1)
    gates = sel / jnp.sum(sel, axis=-1, keepdims=True)
    first_is_lo = loc[:, 0] < loc[:, 1]
    lo = jnp.minimum(loc[:, 0], loc[:, 1])
    hi = jnp.maximum(loc[:, 0], loc[:, 1])
    pair = jnp.where(lo == 0, hi - 1, jnp.where(lo == 1, hi + 1, 5))
    cls = gidx * PAIRS_PER_GROUP + pair
    gate_lo = jnp.where(first_is_lo, gates[:, 0], gates[:, 1])
    gate_hi = jnp.where(first_is_lo, gates[:, 1], gates[:, 0])

    onehot = (cls[:, None] == jnp.arange(N_CLASSES, dtype=jnp.int32)[None, :]).astype(jnp.int32)
    running = jnp.cumsum(onehot, axis=0)
    counts = running[-1]
    rank = jnp.sum(running * onehot, axis=1) - 1
    padded = (counts + MOE_TILE - 1) // MOE_TILE * MOE_TILE
    pad_end = jnp.cumsum(padded)
    pad_start = pad_end - padded
    dest = (pad_start[cls] + rank).astype(jnp.int32)

    n_used = (pad_end[-1] // MOE_TILE).astype(jnp.int32)
    blk_row = jnp.arange(n_blocks, dtype=jnp.int32) * MOE_TILE
    blk_cls = jnp.minimum(jnp.searchsorted(pad_end, blk_row, side="right"), N_CLASSES - 1).astype(jnp.int32)
    blk_grp = blk_cls // PAIRS_PER_GROUP
    blk_pair = blk_cls % PAIRS_PER_GROUP
    blk_e0 = blk_grp * EXPERTS_PER_GROUP + jnp.asarray(_PAIR_LO, jnp.int32)[blk_pair]
    blk_e1 = blk_grp * EXPERTS_PER_GROUP + jnp.asarray(_PAIR_HI, jnp.int32)[blk_pair]
    blk_src = jnp.minimum(jnp.arange(n_blocks, dtype=jnp.int32), n_used - 1)
    fill_lo = (pad_start + counts).astype(jnp.int32)
    fill_hi = pad_end.astype(jnp.int32)
    gate_rows = jnp.zeros((n_blocks * MOE_TILE, 2), F32).at[dest].set(jnp.stack([gate_lo, gate_hi], axis=1))
    return dest, gate_rows, blk_e0, blk_e1, blk_src, n_used.reshape(1), fill_lo, fill_hi


def _row_copy_wait(src_ref, dst_ref, sem, rows):
    pltpu.make_async_copy(src_ref.at[pl.ds(0, rows)], dst_ref.at[pl.ds(0, rows)], sem).wait()


def _zero_fill_copies(fill_lo_ref, fill_hi_ref, nused_ref, zero_ref, out_hbm, sem, n_classes, n_blocks, act):
    for c in range(n_classes):
        lo = fill_lo_ref[c]
        hi = fill_hi_ref[c]
        lo8 = jnp.bitwise_and(lo + 7, -8)

        def single(r, carry):
            act(pltpu.make_async_copy(zero_ref.at[pl.ds(0, 1)], out_hbm.at[pl.ds(r, 1)], sem))
            return carry

        lax.fori_loop(lo, lo8, single, 0)
        pos = lo8
        n = hi - lo8
        bit = 8
        while bit < MOE_TILE:
            take = n & bit

            @pl.when(take != 0)
            def _(pos=pos, bit=bit):
                dst = out_hbm.at[pl.ds(pl.multiple_of(pos, 8), bit)]
                act(pltpu.make_async_copy(zero_ref.at[pl.ds(0, bit)], dst, sem))

            pos = pos + take
            bit *= 2

    def tail(blk, carry):
        row0 = pl.multiple_of(blk * MOE_TILE, MOE_TILE)
        act(pltpu.make_async_copy(zero_ref, out_hbm.at[pl.ds(row0, MOE_TILE)], sem))
        return carry

    lax.fori_loop(nused_ref[0], n_blocks, tail, 0)


def _scatter_rows_kernel(fill_lo_ref, fill_hi_ref, nused_ref, dest_hbm, src_ref, out_hbm,
                         idx_ref, zero_ref, sem, idx_sem, fill_sem, *, rows, n_classes, n_blocks):
    i = pl.program_id(0)
    fill = functools.partial(_zero_fill_copies, fill_lo_ref, fill_hi_ref, nused_ref, zero_ref, out_hbm,
                             fill_sem, n_classes, n_blocks)

    @pl.when(i == 0)
    def _():
        zero_ref[...] = jnp.zeros(zero_ref.shape, F32)
        fill(lambda cp: cp.start())

    idx_copy = pltpu.make_async_copy(dest_hbm.at[pl.ds(i * rows, rows)], idx_ref, idx_sem)
    idx_copy.start()
    idx_copy.wait()

    def issue(r, carry):
        pltpu.make_async_copy(src_ref.at[pl.ds(r, 1)], out_hbm.at[pl.ds(idx_ref[r], 1)], sem).start()
        return carry

    lax.fori_loop(0, rows, issue, 0)
    _row_copy_wait(src_ref, out_hbm, sem, rows)

    @pl.when(i == pl.num_programs(0) - 1)
    def _():
        fill(lambda cp: cp.wait())


def _scatter_rows(src, dest, fill_lo, fill_hi, n_used, n_rows_out):
    n_tok, d = src.shape
    rows = min(PERM_TILE, n_tok)
    kern = functools.partial(_scatter_rows_kernel, rows=rows, n_classes=N_CLASSES,
                             n_blocks=n_rows_out // MOE_TILE)
    grid_spec = pltpu.PrefetchScalarGridSpec(
        num_scalar_prefetch=3,
        grid=(n_tok // rows,),
        in_specs=[pl.BlockSpec(memory_space=pl.ANY),
                  pl.BlockSpec((rows, d), lambda i, *_: (i, 0))],
        out_specs=pl.BlockSpec(memory_space=pl.ANY),
        scratch_shapes=[pltpu.SMEM((rows,), jnp.int32), pltpu.VMEM((MOE_TILE, d), F32),
                        pltpu.SemaphoreType.DMA, pltpu.SemaphoreType.DMA, pltpu.SemaphoreType.DMA],
    )
    return pl.pallas_call(
        kern,
        grid_spec=grid_spec,
        out_shape=jax.ShapeDtypeStruct((n_rows_out, d), F32),
        compiler_params=_cparams("arbitrary"),
        name="moe_scatter_rows",
    )(fill_lo, fill_hi, n_used, dest, src)


def _moe_ffn_kernel(e0_ref, e1_ref, src_ref, nused_ref, x_ref, g_ref,
                    w1a_ref, w3a_ref, w2a_ref, w1b_ref, w3b_ref, w2b_ref, o_ref):
    i = pl.program_id(0)

    @pl.when(i < nused_ref[0])
    def _():
        xb = x_ref[...].astype(BF16)

        def expert(w1_ref, w3_ref, w2_ref):
            a = _dot(xb, w1_ref[...])
            b = _dot(xb, w3_ref[...])
            return _dot((a * jax.nn.sigmoid(a) * b).astype(BF16), w2_ref[...])

        o_ref[...] = (expert(w1a_ref, w3a_ref, w2a_ref) * g_ref[:, 0:1]
                      + expert(w1b_ref, w3b_ref, w2b_ref) * g_ref[:, 1:2])

    @pl.when(i >= nused_ref[0])
    def _():
        o_ref[...] = jnp.zeros(o_ref.shape, F32)


def _moe_ffn(sorted_h, gate_rows, blk_e0, blk_e1, blk_src, n_used, w1, w3, w2):
    n_rows, d = sorted_h.shape
    f = w1.shape[-1]
    n_blocks = n_rows // MOE_TILE
    wa = lambda i, e0, e1, src, nu: (e0[i], 0, 0)
    wb = lambda i, e0, e1, src, nu: (e1[i], 0, 0)
    grid_spec = pltpu.PrefetchScalarGridSpec(
        num_scalar_prefetch=4,
        grid=(n_blocks,),
        in_specs=[pl.BlockSpec((MOE_TILE, d), lambda i, e0, e1, src, nu: (src[i], 0)),
                  pl.BlockSpec((MOE_TILE, 2), lambda i, e0, e1, src, nu: (src[i], 0)),
                  pl.BlockSpec((None, d, f), wa), pl.BlockSpec((None, d, f), wa), pl.BlockSpec((None, f, d), wa),
                  pl.BlockSpec((None, d, f), wb), pl.BlockSpec((None, d, f), wb), pl.BlockSpec((None, f, d), wb)],
        out_specs=pl.BlockSpec((MOE_TILE, d), lambda i, e0, e1, src, nu: (i, 0)),
    )
    w1b, w3b, w2b = w1.astype(BF16), w3.astype(BF16), w2.astype(BF16)
    return pl.pallas_call(
        _moe_ffn_kernel,
        grid_spec=grid_spec,
        out_shape=jax.ShapeDtypeStruct((n_rows, d), F32),
        compiler_params=_cparams("arbitrary"),
        name="moe_pair_ffn",
    )(blk_e0, blk_e1, blk_src, n_used, sorted_h, gate_rows, w1b, w3b, w2b, w1b, w3b, w2b)


def _gather_residual_kernel(dest_hbm, x_ref, mod_ref, y_hbm, o_ref, idx_ref, rows_ref, sem, idx_sem, *, rows):
    b = pl.program_id(0)
    i = pl.program_id(1)
    base = (b * pl.num_programs(1) + i) * rows
    idx_copy = pltpu.make_async_copy(dest_hbm.at[pl.ds(base, rows)], idx_ref, idx_sem)
    idx_copy.start()
    idx_copy.wait()

    def issue(r, carry):
        pltpu.make_async_copy(y_hbm.at[pl.ds(idx_ref[r], 1)], rows_ref.at[pl.ds(r, 1)], sem).start()
        return carry

    lax.fori_loop(0, rows, issue, 0)
    _row_copy_wait(y_hbm, rows_ref, sem, rows)
    o_ref[...] = x_ref[...] + mod_ref[5:6, :] * rows_ref[...]


def _gather_residual(x, mod, sorted_y, dest):
    bsz, seq, d = x.shape
    rows = min(PERM_TILE, seq)
    kern = functools.partial(_gather_residual_kernel, rows=rows)
    return pl.pallas_call(
        kern,
        grid=(bsz, seq // rows),
        in_specs=[pl.BlockSpec(memory_space=pl.ANY),
                  pl.BlockSpec((None, rows, d), lambda b, i: (b, i, 0)),
                  pl.BlockSpec((None, 6, d), lambda b, i: (b, 0, 0)),
                  pl.BlockSpec(memory_space=pl.ANY)],
        out_specs=pl.BlockSpec((None, rows, d), lambda b, i: (b, i, 0)),
        out_shape=jax.ShapeDtypeStruct((bsz, seq, d), F32),
        scratch_shapes=[pltpu.SMEM((rows,), jnp.int32), pltpu.VMEM((rows, d), F32),
                        pltpu.SemaphoreType.DMA, pltpu.SemaphoreType.DMA],
        compiler_params=_cparams("arbitrary", "arbitrary"),
        name="moe_gather_residual",
    )(dest, x, mod, sorted_y)


def _moe_layer(x, norm_g, mod, router_w, router_b, w1, w3, w2):
    bsz, seq, d = x.shape
    n_tok = bsz * seq
    n_blocks = -(-n_tok // MOE_TILE) + N_CLASSES
    h, scores = _router(x, norm_g, mod, router_w)
    dest, gate_rows, blk_e0, blk_e1, blk_src, n_used, fill_lo, fill_hi = _routing_tables(
        scores, router_b, n_blocks)
    sorted_h = _scatter_rows(h, dest, fill_lo, fill_hi, n_used, n_blocks * MOE_TILE)
    sorted_y = _moe_ffn(sorted_h, gate_rows, blk_e0, blk_e1, blk_src, n_used, w1, w3, w2)
    return _gather_residual(x, mod, sorted_y, dest)


def kernel(x, c, mod_w, mod_b, mix_norm_g, ffn_norm_g, ev_in_w, ev_dw_w, ev_dw_b, ev_ln_g, ev_ln_b,
           ev_q_g, ev_k_g, ev_out_w, od_in_w, od_conv_w, od_conv_b, od_rg_w, od_rg_b, od_ig_w, od_ig_b,
           od_lam, od_out_w, router_w, router_b, ex_w1, ex_w3, ex_w2):
    depth = mod_w.shape[0]
    mod = _modulation(c, mod_w, mod_b)
    for layer in range(depth):
        m = mod[layer]
        if layer % 2 == 0:
            e = layer // 2
            conv_ch = ev_dw_w.shape[-1]
            a, q, k, v = _inproj_even(x, mix_norm_g[layer], m, ev_in_w[e], ev_q_g[e], ev_k_g[e], conv_ch)
            u = _conv_module(a, ev_dw_w[e], ev_dw_b[e], ev_ln_g[e], ev_ln_b[e])
            o = _sb_attention(q, k, v, ev_q_g.shape[-1])
            x = _outproj_residual(x, m, [u, o], [ev_out_w[e][:conv_ch], ev_out_w[e][conv_ch:]])
        else:
            o = layer // 2
            proj = _inproj_odd(x, mix_norm_g[layer], m, od_in_w[o])
            mixed = _rglru(proj, od_conv_w[o], od_conv_b[o], od_rg_w[o], od_rg_b[o], od_ig_w[o], od_ig_b[o],
                           od_lam[o])
            x = _outproj_residual(x, m, [mixed], [od_out_w[o]])
        x = _moe_layer(x, ffn_norm_g[layer], m, router_w, router_b, ex_w1[layer], ex_w3[layer], ex_w2[layer])
    return x
```

```python
import functools
import math

import jax
import jax.numpy as jnp
from jax import lax
from jax.experimental import pallas as pl
from jax.experimental.pallas import tpu as pltpu

F32 = jnp.float32
BF16 = jnp.bfloat16
EPS = 1e-6
LOG2E = 1.4426950408889634
LRU_C = 8.0
N_GROUPS = 4
EXPERTS_PER_GROUP = 4
PAIRS_PER_GROUP = 6
N_CLASSES = N_GROUPS * PAIRS_PER_GROUP
_PAIR_LO = (0, 0, 0, 1, 1, 2)
_PAIR_HI = (1, 2, 3, 2, 3, 3)

V7X_VMEM_LIMIT_BYTES = 56 * 1024 * 1024
ROW_TILE = 512
SEQ_TILE = 256
ATT_TILE = 256
MOE_TILE = 256
PERM_TILE = 1024


def _cparams(*sem):
    return pltpu.CompilerParams(dimension_semantics=sem, vmem_limit_bytes=V7X_VMEM_LIMIT_BYTES)


def _dot(a, b):
    return jnp.dot(a, b, preferred_element_type=F32)


def _dot_nt(a, b):
    return lax.dot_general(a, b, (((1,), (1,)), ((), ())), preferred_element_type=F32)


def _split_bf16(x):
    hi = x.astype(BF16)
    lo = (x - hi.astype(F32)).astype(BF16)
    return hi, lo


def _neg_softplus(z):
    return -(jnp.maximum(z, 0.0) + jnp.log1p(jnp.exp(-jnp.abs(z))))


def _norm_mod(x, g, shift, scale):
    ms = jnp.mean(x * x, axis=-1, keepdims=True)
    y = x * lax.rsqrt(ms + EPS) * g
    return y * (1.0 + scale) + shift


def _mod_kernel(c_ref, w_ref, b_ref, o_ref):
    c = c_ref[...]
    cond = (c * jax.nn.sigmoid(c)).astype(BF16)
    o_ref[...] = _dot(cond, w_ref[...].astype(BF16)) + b_ref[...]


def _modulation(c, mod_w, mod_b):
    depth, d, n = mod_w.shape
    bsz = c.shape[0]
    tn = min(n, 1536)
    out = pl.pallas_call(
        _mod_kernel,
        grid=(depth, n // tn),
        in_specs=[pl.BlockSpec((bsz, d), lambda l, j: (0, 0)),
                  pl.BlockSpec((None, d, tn), lambda l, j: (l, 0, j)),
                  pl.BlockSpec((None, 1, tn), lambda l, j: (l, 0, j))],
        out_specs=pl.BlockSpec((None, bsz, tn), lambda l, j: (l, 0, j)),
        out_shape=jax.ShapeDtypeStruct((depth, bsz, n), F32),
        compiler_params=_cparams("arbitrary", "arbitrary"),
        name="adaln_mod",
    )(c, mod_w, mod_b.reshape(depth, 1, n))
    return out.reshape(depth, bsz, 6, d)


def _inproj_even_kernel(x_ref, g_ref, mod_ref, w_ref, e_ref, qg_ref, kg_ref,
                        a_ref, q_ref, k_ref, v_ref, *, conv2, sbw, head_dim):
    h = _norm_mod(x_ref[...], g_ref[...], mod_ref[0:1, :], mod_ref[1:2, :]).astype(BF16)
    a_ref[...] = _dot(h, w_ref[:, 0:conv2])

    def head_rms(t, gain):
        hi, lo = _split_bf16(t * t)
        ss = _dot(hi, e_ref[...]) + _dot(lo, e_ref[...])
        return t * lax.rsqrt(ss * (1.0 / head_dim) + EPS) * gain

    q = _dot(h, w_ref[:, conv2:conv2 + sbw])
    q_ref[...] = (head_rms(q, qg_ref[...]) * (LOG2E / math.sqrt(head_dim))).astype(BF16)
    k = _dot(h, w_ref[:, conv2 + sbw:conv2 + 2 * sbw])
    k_ref[...] = head_rms(k, kg_ref[...]).astype(BF16)
    v_ref[...] = _dot(h, w_ref[:, conv2 + 2 * sbw:conv2 + 3 * sbw]).astype(BF16)


def _inproj_even(x, norm_g, mod, in_w, q_g, k_g, conv_ch):
    bsz, seq, d = x.shape
    n = in_w.shape[1]
    conv2 = 2 * conv_ch
    sbw = (n - conv2) // 3
    head_dim = q_g.shape[0]
    heads = sbw // head_dim
    tm = min(ROW_TILE, seq)
    head_sum = jnp.kron(jnp.eye(heads, dtype=F32), jnp.ones((head_dim, head_dim), F32)).astype(BF16)
    kern = functools.partial(_inproj_even_kernel, conv2=conv2, sbw=sbw, head_dim=head_dim)
    row = lambda b, i: (b, i, 0)
    const = lambda b, i: (0, 0)
    return pl.pallas_call(
        kern,
        grid=(bsz, seq // tm),
        in_specs=[pl.BlockSpec((None, tm, d), row),
                  pl.BlockSpec((1, d), const),
                  pl.BlockSpec((None, 6, d), lambda b, i: (b, 0, 0)),
                  pl.BlockSpec((d, n), const),
                  pl.BlockSpec((sbw, sbw), const),
                  pl.BlockSpec((1, sbw), const),
                  pl.BlockSpec((1, sbw), const)],
        out_specs=[pl.BlockSpec((None, tm, conv2), row),
                   pl.BlockSpec((None, tm, sbw), row),
                   pl.BlockSpec((None, tm, sbw), row),
                   pl.BlockSpec((None, tm, sbw), row)],
        out_shape=[jax.ShapeDtypeStruct((bsz, seq, conv2), F32),
                   jax.ShapeDtypeStruct((bsz, seq, sbw), BF16),
                   jax.ShapeDtypeStruct((bsz, seq, sbw), BF16),
                   jax.ShapeDtypeStruct((bsz, seq, sbw), BF16)],
        compiler_params=_cparams("arbitrary", "arbitrary"),
        name="even_in_proj",
    )(x, norm_g.reshape(1, d), mod, in_w.astype(BF16), head_sum,
      jnp.tile(q_g, heads).reshape(1, sbw), jnp.tile(k_g, heads).reshape(1, sbw))


def _conv_module_kernel(a_ref, w_ref, b_ref, lg_ref, lb_ref, o_ref, buf_ref, sh_ref, *, ch, width, halo, ts, chunk):
    si = pl.program_id(1)
    span = halo + ts - 8

    @pl.when(si == 0)
    def _():
        buf_ref[0:halo, :] = jnp.zeros((halo, ch), F32)

    @pl.when(si > 0)
    def _():
        buf_ref[0:halo, :] = buf_ref[ts:ts + halo, :]

    val = a_ref[:, 0:ch]
    gate = a_ref[:, ch:2 * ch]
    buf_ref[halo:halo + ts, :] = val * jax.nn.sigmoid(gate)

    for p in range(1, 8):
        sh_ref[p, 0:span, :] = buf_ref[p:p + span, :]

    off = halo - (width - 1)
    for c in range(ts // chunk):
        acc = jnp.broadcast_to(b_ref[...], (chunk, ch))
        for k in range(width):
            p = (off + k) % 8
            r0 = c * chunk + off + k - p
            tap = buf_ref[r0:r0 + chunk, :] if p == 0 else sh_ref[p, r0:r0 + chunk, :]
            acc = acc + w_ref[k:k + 1, :] * tap
        mu = jnp.mean(acc, axis=-1, keepdims=True)
        xc = acc - mu
        var = jnp.mean(xc * xc, axis=-1, keepdims=True)
        y = xc * lax.rsqrt(var + EPS) * lg_ref[...] + lb_ref[...]
        o_ref[c * chunk:(c + 1) * chunk, :] = (y * jax.nn.sigmoid(y)).astype(BF16)


def _conv_module(a, dw_w, dw_b, ln_g, ln_b):
    bsz, seq, ch2 = a.shape
    ch = ch2 // 2
    width = dw_w.shape[0]
    halo = -(-(width - 1) // 8) * 8
    ts = min(SEQ_TILE, seq)
    chunk = min(32, ts)
    kern = functools.partial(_conv_module_kernel, ch=ch, width=width, halo=halo, ts=ts, chunk=chunk)
    const = lambda b, i: (0, 0)
    return pl.pallas_call(
        kern,
        grid=(bsz, seq // ts),
        in_specs=[pl.BlockSpec((None, ts, ch2), lambda b, i: (b, i, 0)),
                  pl.BlockSpec((width, ch), const),
                  pl.BlockSpec((1, ch), const),
                  pl.BlockSpec((1, ch), const),
                  pl.BlockSpec((1, ch), const)],
        out_specs=pl.BlockSpec((None, ts, ch), lambda b, i: (b, i, 0)),
        out_shape=jax.ShapeDtypeStruct((bsz, seq, ch), BF16),
        scratch_shapes=[pltpu.VMEM((halo + ts, ch), F32), pltpu.VMEM((8, halo + ts, ch), F32)],
        compiler_params=_cparams("arbitrary", "arbitrary"),
        name="conformer_conv",
    )(a, dw_w, dw_b.reshape(1, ch), ln_g.reshape(1, ch), ln_b.reshape(1, ch))


def _sb_attention_kernel(q_ref, k_ref, v_ref, u_ref, o_ref, *, head_dim, tq):
    qi = pl.program_id(2)
    n_heads = q_ref.shape[1] // head_dim
    head_lanes = [slice(hh * head_dim, (hh + 1) * head_dim) for hh in range(n_heads)]
    rows = lax.broadcasted_iota(jnp.int32, (tq, tq), 0)
    cols = lax.broadcasted_iota(jnp.int32, (tq, tq), 1)
    causal = cols < rows

    def tiles(specs, carry):
        z, sp, sp_b = {}, {}, {}
        order = [(ti, hh) for ti in range(len(specs)) for hh in range(n_heads)]
        for ti, hh in order:
            k0, mask = specs[ti]
            zz = _dot_nt(q_ref[:, head_lanes[hh]], k_ref[pl.ds(k0, tq), head_lanes[hh]])
            ss = jnp.maximum(zz, 0.0) + jnp.log(1.0 + jnp.exp2(-jnp.abs(zz))) * LOG2E
            if mask is not None:
                ss = jnp.where(mask, ss, 0.0)
            z[ti, hh], sp[ti, hh], sp_b[ti, hh] = zz, ss, ss.astype(BF16)
        later_all = _dot(jnp.concatenate([sp_b[key] for key in order], axis=0), u_ref[...])
        carry = list(carry)
        for idx, (ti, hh) in enumerate(order):
            k0, mask = specs[ti]
            run, acc = carry[2 * hh], carry[2 * hh + 1]
            later = later_all[idx * tq:(idx + 1) * tq, :]
            w = jnp.exp2(z[ti, hh] - sp[ti, hh] - later - run)
            if mask is not None:
                w = jnp.where(mask, w, 0.0)
            carry[2 * hh + 1] = acc + _dot(w.astype(BF16), v_ref[pl.ds(k0, tq), head_lanes[hh]])
            carry[2 * hh] = run + (later[:, 0:1] + sp_b[ti, hh][:, 0:1].astype(F32))
        return tuple(carry)

    def tile_start(t):
        return pl.multiple_of(t * tq, tq)

    has_prev = jnp.broadcast_to(qi > 0, (tq, tq))
    init = (jnp.zeros((tq, 1), F32), jnp.zeros((tq, head_dim), F32)) * n_heads
    carry = tiles([(tile_start(qi), causal), (tile_start(jnp.maximum(qi - 1, 0)), has_prev)], init)

    def pair_body(step, carry):
        t = qi - 2 - 2 * step
        return tiles([(tile_start(t), None), (tile_start(t - 1), None)], carry)

    def single_body(step, carry):
        return tiles([(0, None)], carry)

    left = jnp.maximum(qi - 1, 0)
    carry = lax.fori_loop(0, lax.shift_right_logical(left, 1), pair_body, carry)
    carry = lax.fori_loop(0, jnp.bitwise_and(left, 1), single_body, carry)
    for hh, lanes in enumerate(head_lanes):
        o_ref[:, lanes] = carry[2 * hh + 1].astype(BF16)


def _sb_attention(q, k, v, head_dim):
    bsz, seq, sbw = q.shape
    tq = min(ATT_TILE, seq)
    lane_blk = min(128, sbw)
    later = (lax.broadcasted_iota(jnp.int32, (tq, tq), 0) > lax.broadcasted_iota(jnp.int32, (tq, tq), 1))
    suffix = later.astype(BF16)
    kern = functools.partial(_sb_attention_kernel, head_dim=head_dim, tq=tq)
    return pl.pallas_call(
        kern,
        grid=(bsz, sbw // lane_blk, seq // tq),
        in_specs=[pl.BlockSpec((None, tq, lane_blk), lambda b, h, i: (b, i, h)),
                  pl.BlockSpec((None, seq, lane_blk), lambda b, h, i: (b, 0, h)),
                  pl.BlockSpec((None, seq, lane_blk), lambda b, h, i: (b, 0, h)),
                  pl.BlockSpec((tq, tq), lambda b, h, i: (0, 0))],
        out_specs=pl.BlockSpec((None, tq, lane_blk), lambda b, h, i: (b, i, h)),
        out_shape=jax.ShapeDtypeStruct((bsz, seq, sbw), BF16),
        compiler_params=_cparams("arbitrary", "arbitrary", "arbitrary"),
        name="stickbreak_attn",
    )(q, k, v, suffix)


def _outproj_kernel(*refs, n_in):
    x_ref, mod_ref = refs[0], refs[1]
    ins = refs[2:2 + n_in]
    ws = refs[2 + n_in:2 + 2 * n_in]
    o_ref = refs[2 + 2 * n_in]
    acc = _dot(ins[0][...], ws[0][...])
    for t_ref, w_ref in zip(ins[1:], ws[1:]):
        acc = acc + _dot(t_ref[...], w_ref[...])
    o_ref[...] = x_ref[...] + mod_ref[2:3, :] * acc


def _outproj_residual(x, mod, parts, weights):
    bsz, seq, d = x.shape
    tm = min(ROW_TILE, seq)
    row = lambda b, i: (b, i, 0)
    const = lambda b, i: (0, 0)
    in_specs = [pl.BlockSpec((None, tm, d), row), pl.BlockSpec((None, 6, d), lambda b, i: (b, 0, 0))]
    in_specs += [pl.BlockSpec((None, tm, p.shape[-1]), row) for p in parts]
    in_specs += [pl.BlockSpec(w.shape, const) for w in weights]
    return pl.pallas_call(
        functools.partial(_outproj_kernel, n_in=len(parts)),
        grid=(bsz, seq // tm),
        in_specs=in_specs,
        out_specs=pl.BlockSpec((None, tm, d), row),
        out_shape=jax.ShapeDtypeStruct((bsz, seq, d), F32),
        compiler_params=_cparams("arbitrary", "arbitrary"),
        name="out_proj_residual",
    )(x, mod, *parts, *[w.astype(BF16) for w in weights])


def _inproj_odd_kernel(x_ref, g_ref, mod_ref, w_ref, o_ref, *, n, tn):
    h = _norm_mod(x_ref[...], g_ref[...], mod_ref[0:1, :], mod_ref[1:2, :]).astype(BF16)
    for j in range(n // tn):
        o_ref[:, j * tn:(j + 1) * tn] = _dot(h, w_ref[:, j * tn:(j + 1) * tn])


def _inproj_odd(x, norm_g, mod, in_w):
    bsz, seq, d = x.shape
    n = in_w.shape[1]
    tm = min(ROW_TILE, seq)
    tn = min(512, n)
    row = lambda b, i: (b, i, 0)
    const = lambda b, i: (0, 0)
    return pl.pallas_call(
        functools.partial(_inproj_odd_kernel, n=n, tn=tn),
        grid=(bsz, seq // tm),
        in_specs=[pl.BlockSpec((None, tm, d), row),
                  pl.BlockSpec((1, d), const),
                  pl.BlockSpec((None, 6, d), lambda b, i: (b, 0, 0)),
                  pl.BlockSpec((d, n), const)],
        out_specs=pl.BlockSpec((None, tm, n), row),
        out_shape=jax.ShapeDtypeStruct((bsz, seq, n), F32),
        compiler_params=_cparams("arbitrary", "arbitrary"),
        name="odd_in_proj",
    )(x, norm_g.reshape(1, d), mod, in_w.astype(BF16))


def _rglru_kernel(y_ref, x_ref, cw_ref, cb_ref, gw_ref, rb_ref, ib_ref, lam_ref, o_ref,
                  xbuf_ref, hprev_ref, *, ts, width, bs, nb):
    si = pl.program_id(1)
    halo = 8
    lw = nb * bs

    @pl.when(si == 0)
    def _():
        xbuf_ref[0:halo, :] = jnp.zeros((halo, lw), F32)
        hprev_ref[...] = jnp.zeros((8, lw), F32)

    @pl.when(si > 0)
    def _():
        xbuf_ref[0:halo, :] = xbuf_ref[ts:ts + halo, :]

    xbuf_ref[halo:halo + ts, :] = x_ref[...]
    sub = lax.broadcasted_iota(jnp.int32, (8, bs), 0)
    off = halo - (width - 1)

    for nblk in range(nb):
        lanes = slice(nblk * bs, (nblk + 1) * bs)
        xc = jnp.broadcast_to(cb_ref[:, lanes], (ts, bs))
        for k in range(width):
            xc = xc + cw_ref[k:k + 1, lanes] * xbuf_ref[off + k:off + k + ts, lanes]
        gates = _dot(xc.astype(BF16), gw_ref[nblk])
        r = jax.nn.sigmoid(gates[:, 0:bs] + rb_ref[:, lanes])
        ig = jax.nn.sigmoid(gates[:, bs:2 * bs] + ib_ref[:, lanes])
        log_a = LRU_C * r * _neg_softplus(-lam_ref[:, lanes])
        a = jnp.exp(log_a)
        mult = jnp.sqrt(-jnp.tanh(log_a) * (a * a + 1.0))
        b_in = mult * (ig * xc)

        h_last = hprev_ref[0:1, lanes]
        yv = y_ref[:, lanes]
        for g in range(ts // 8):
            ag = a[g * 8:(g + 1) * 8, :]
            bg = b_in[g * 8:(g + 1) * 8, :]
            for d in (1, 2, 4):
                a_sh = jnp.where(sub >= d, pltpu.roll(ag, d, 0), 1.0)
                b_sh = jnp.where(sub >= d, pltpu.roll(bg, d, 0), 0.0)
                bg = ag * b_sh + bg
                ag = ag * a_sh
            hg = ag * h_last + bg
            h_last = hg[7:8, :]
            yg = yv[g * 8:(g + 1) * 8, :]
            gelu = 0.5 * yg * (1.0 + jnp.tanh(math.sqrt(2.0 / math.pi) * (yg + 0.044715 * (yg * yg * yg))))
            o_ref[g * 8:(g + 1) * 8, lanes] = (gelu * hg).astype(BF16)
        hprev_ref[0:1, lanes] = h_last


def _rglru(proj, conv_w, conv_b, rg_w, rg_b, ig_w, ig_b, lam):
    bsz, seq, n2 = proj.shape
    lw = n2 // 2
    nb, bs, _ = rg_w.shape
    width = conv_w.shape[0]
    ts = min(SEQ_TILE, seq)
    gate_w = jnp.concatenate([rg_w, ig_w], axis=-1).astype(BF16)
    kern = functools.partial(_rglru_kernel, ts=ts, width=width, bs=bs, nb=nb)
    const = lambda b, i: (0, 0)
    return pl.pallas_call(
        kern,
        grid=(bsz, seq // ts),
        in_specs=[pl.BlockSpec((None, ts, lw), lambda b, i: (b, i, 0)),
                  pl.BlockSpec((None, ts, lw), lambda b, i: (b, i, 1)),
                  pl.BlockSpec((width, lw), const),
                  pl.BlockSpec((1, lw), const),
                  pl.BlockSpec((nb, bs, 2 * bs), lambda b, i: (0, 0, 0)),
                  pl.BlockSpec((1, lw), const),
                  pl.BlockSpec((1, lw), const),
                  pl.BlockSpec((1, lw), const)],
        out_specs=pl.BlockSpec((None, ts, lw), lambda b, i: (b, i, 0)),
        out_shape=jax.ShapeDtypeStruct((bsz, seq, lw), BF16),
        scratch_shapes=[pltpu.VMEM((8 + ts, lw), F32), pltpu.VMEM((8, lw), F32)],
        compiler_params=_cparams("arbitrary", "arbitrary"),
        name="rglru",
    )(proj, proj, conv_w, conv_b.reshape(1, lw), gate_w, rg_b.reshape(1, lw), ig_b.reshape(1, lw),
      lam.reshape(1, lw))


CLASS_ROWS = 32
META_ROWS = 8
GATE_LANES = 128


def _first_argmax(vals):
    best, idx = vals[0], jnp.zeros_like(vals[0])
    for j in range(1, len(vals)):
        upd = vals[j] > best
        best = jnp.where(upd, vals[j], best)
        idx = jnp.where(upd, float(j), idx)
    return idx, best


def _pick(idx, vals):
    out = vals[0]
    for j in range(1, len(vals)):
        out = jnp.where(idx == float(j), vals[j], out)
    return out


def _router_kernel(x_ref, g_ref, mod_ref, rwt_ref, rb_ref, tri_ref, h_ref, meta_ref, cnt_ref, base_ref,
                   *, d, tm):
    @pl.when((pl.program_id(0) == 0) & (pl.program_id(1) == 0))
    def _():
        base_ref[...] = jnp.zeros(base_ref.shape, F32)

    h = _norm_mod(x_ref[...], g_ref[...], mod_ref[3:4, :], mod_ref[4:5, :])
    h_ref[:, 0:d] = h
    hi, lo = _split_bf16(h)
    whi, wlo = _split_bf16(rwt_ref[...])
    logits = _dot_nt(whi, hi) + (_dot_nt(whi, lo) + _dot_nt(wlo, hi))
    scores = jax.nn.sigmoid(logits)
    biased = scores + rb_ref[...]
    ne = N_GROUPS * EXPERTS_PER_GROUP
    s_rows = [scores[e:e + 1, :] for e in range(ne)]
    b_rows = [biased[e:e + 1, :] for e in range(ne)]

    group_scores = []
    for g in range(N_GROUPS):
        v = b_rows[g * EXPERTS_PER_GROUP:(g + 1) * EXPERTS_PER_GROUP]
        top2 = v[_PAIR_LO[0]] + v[_PAIR_HI[0]]
        for lo_i, hi_i in zip(_PAIR_LO[1:], _PAIR_HI[1:]):
            top2 = jnp.maximum(top2, v[lo_i] + v[hi_i])
        group_scores.append(top2)
    gidx, _ = _first_argmax(group_scores)
    in_b = [_pick(gidx, [b_rows[g * EXPERTS_PER_GROUP + j] for g in range(N_GROUPS)])
            for j in range(EXPERTS_PER_GROUP)]
    in_s = [_pick(gidx, [s_rows[g * EXPERTS_PER_GROUP + j] for g in range(N_GROUPS)])
            for j in range(EXPERTS_PER_GROUP)]
    i1, _ = _first_argmax(in_b)
    i2, _ = _first_argmax([jnp.where(i1 == float(j), -jnp.inf, in_b[j]) for j in range(EXPERTS_PER_GROUP)])
    sel1 = _pick(i1, in_s)
    sel2 = _pick(i2, in_s)
    total = sel1 + sel2
    gate1 = sel1 / total
    gate2 = sel2 / total
    first_lo = i1 < i2
    e_lo = jnp.minimum(i1, i2)
    e_hi = jnp.maximum(i1, i2)
    pair = jnp.where(e_lo == 0.0, e_hi - 1.0, jnp.where(e_lo == 1.0, e_hi + 1.0, 5.0))
    cls = gidx * float(PAIRS_PER_GROUP) + pair
    gate_lo = jnp.where(first_lo, gate1, gate2)
    gate_hi = jnp.where(first_lo, gate2, gate1)

    class_id = lax.broadcasted_iota(jnp.int32, (CLASS_ROWS, tm), 0).astype(F32)
    onehot = (class_id == cls).astype(F32)
    counted = _dot(onehot.astype(BF16), tri_ref[...])
    rank = jnp.sum(onehot * (counted[:, 0:tm] + base_ref[...]), axis=0, keepdims=True)
    base_ref[...] = base_ref[...] + counted[:, tm:2 * tm]
    cnt_ref[...] = base_ref[:, 0:128]

    zeros = jnp.zeros((1, tm), F32)
    meta_ref[...] = jnp.concatenate([cls, gate_lo, gate_hi, rank] + [zeros] * (META_ROWS - 4), axis=0)
    gate_cols = jnp.concatenate([gate_lo, gate_hi, jnp.zeros((GATE_LANES - 2, tm), F32)], axis=0)
    h_ref[:, d:d + GATE_LANES] = gate_cols.T


def _router(x, norm_g, mod, router_w, router_b):
    bsz, seq, d = x.shape
    ne = router_w.shape[1]
    tm = min(ROW_TILE, seq)
    earlier = (lax.broadcasted_iota(jnp.int32, (tm, tm), 0) < lax.broadcasted_iota(jnp.int32, (tm, tm), 1))
    tri = jnp.concatenate([earlier.astype(BF16), jnp.ones((tm, tm), BF16)], axis=1)
    row = lambda b, i: (b, i, 0)
    const = lambda b, i: (0, 0)
    kern = functools.partial(_router_kernel, d=d, tm=tm)
    rows, meta, counts = pl.pallas_call(
        kern,
        grid=(bsz, seq // tm),
        in_specs=[pl.BlockSpec((None, tm, d), row),
                  pl.BlockSpec((1, d), const),
                  pl.BlockSpec((None, 6, d), lambda b, i: (b, 0, 0)),
                  pl.BlockSpec((ne, d), const),
                  pl.BlockSpec((ne, 1), const),
                  pl.BlockSpec((tm, 2 * tm), const)],
        out_specs=[pl.BlockSpec((None, tm, d + GATE_LANES), row),
                   pl.BlockSpec((None, META_ROWS, tm), lambda b, i: (b, 0, i)),
                   pl.BlockSpec((CLASS_ROWS, 128), const)],
        out_shape=[jax.ShapeDtypeStruct((bsz, seq, d + GATE_LANES), F32),
                   jax.ShapeDtypeStruct((bsz, META_ROWS, seq), F32),
                   jax.ShapeDtypeStruct((CLASS_ROWS, 128), F32)],
        scratch_shapes=[pltpu.VMEM((CLASS_ROWS, tm), F32)],
        compiler_params=_cparams("arbitrary", "arbitrary"),
        name="moe_router",
    )(x, norm_g.reshape(1, d), mod, router_w.T, router_b.reshape(ne, 1).astype(F32), tri)
    return rows.reshape(bsz * seq, d + GATE_LANES), meta, counts


def _routing_tables(meta, counts, n_blocks):
    bsz, _, seq = meta.shape
    cls = meta[:, 0, :].reshape(bsz * seq).astype(jnp.int32)
    rank = meta[:, 3, :].reshape(bsz * seq).astype(jnp.int32)
    counts = counts[:N_CLASSES, 0].astype(jnp.int32)
    padded = (counts + MOE_TILE - 1) // MOE_TILE * MOE_TILE
    pad_end = jnp.cumsum(padded)
    pad_start = pad_end - padded
    class_ids = jnp.arange(N_CLASSES, dtype=jnp.int32)
    dest = rank + jnp.sum(jnp.where(cls[:, None] == class_ids[None, :], pad_start[None, :], 0), axis=1)

    n_used = (pad_end[-1] // MOE_TILE).astype(jnp.int32)
    blk_row = jnp.arange(n_blocks, dtype=jnp.int32) * MOE_TILE
    blk_cls = jnp.minimum(jnp.sum((pad_end[None, :] <= blk_row[:, None]).astype(jnp.int32), axis=1), N_CLASSES - 1)
    blk_grp = blk_cls // PAIRS_PER_GROUP
    blk_pair = blk_cls % PAIRS_PER_GROUP
    pair_ids = jnp.arange(PAIRS_PER_GROUP, dtype=jnp.int32)
    pick = lambda table: jnp.sum(jnp.where(blk_pair[:, None] == pair_ids[None, :],
                                           jnp.asarray(table, jnp.int32)[None, :], 0), axis=1)
    blk_e0 = blk_grp * EXPERTS_PER_GROUP + pick(_PAIR_LO)
    blk_e1 = blk_grp * EXPERTS_PER_GROUP + pick(_PAIR_HI)
    blk_src = jnp.minimum(jnp.arange(n_blocks, dtype=jnp.int32), n_used - 1)
    fill_lo = (pad_start + counts).astype(jnp.int32)
    fill_hi = pad_end.astype(jnp.int32)
    return dest.astype(jnp.int32), blk_e0, blk_e1, blk_src, n_used.reshape(1), fill_lo, fill_hi


ROW_COPY_UNROLL = 8


def _start_row_copies(make_copy, rows):
    def group(g, carry):
        for j in range(ROW_COPY_UNROLL):
            make_copy(g * ROW_COPY_UNROLL + j).start(priority=j % 2)
        return carry

    lax.fori_loop(0, rows // ROW_COPY_UNROLL, group, 0)


def _row_copy_wait(src_ref, dst_ref, sem, rows):
    pltpu.make_async_copy(src_ref.at[pl.ds(0, rows)], dst_ref.at[pl.ds(0, rows)], sem).wait()


def _zero_fill_copies(fill_lo_ref, fill_hi_ref, nused_ref, zero_ref, out_hbm, sem, n_classes, n_blocks, act):
    for c in range(n_classes):
        lo = fill_lo_ref[c]
        hi = fill_hi_ref[c]
        lo8 = jnp.bitwise_and(lo + 7, -8)

        def single(r, carry):
            act(pltpu.make_async_copy(zero_ref.at[pl.ds(0, 1)], out_hbm.at[pl.ds(r, 1)], sem))
            return carry

        lax.fori_loop(lo, lo8, single, 0)
        pos = lo8
        n = hi - lo8
        bit = 8
        while bit < MOE_TILE:
            take = n & bit

            @pl.when(take != 0)
            def _(pos=pos, bit=bit):
                dst = out_hbm.at[pl.ds(pl.multiple_of(pos, 8), bit)]
                act(pltpu.make_async_copy(zero_ref.at[pl.ds(0, bit)], dst, sem))

            pos = pos + take
            bit *= 2

    def tail(blk, carry):
        row0 = pl.multiple_of(blk * MOE_TILE, MOE_TILE)
        act(pltpu.make_async_copy(zero_ref, out_hbm.at[pl.ds(row0, MOE_TILE)], sem))
        return carry

    lax.fori_loop(nused_ref[0], n_blocks, tail, 0)


def _scatter_rows_kernel(fill_lo_ref, fill_hi_ref, nused_ref, dest_hbm, src_ref, out_hbm,
                         idx_ref, zero_ref, sem, idx_sem, fill_sem, *, rows, n_classes, n_blocks):
    i = pl.program_id(0)
    fill = functools.partial(_zero_fill_copies, fill_lo_ref, fill_hi_ref, nused_ref, zero_ref, out_hbm,
                             fill_sem, n_classes, n_blocks)

    @pl.when(i == 0)
    def _():
        zero_ref[...] = jnp.zeros(zero_ref.shape, F32)
        fill(lambda cp: cp.start())

    idx_copy = pltpu.make_async_copy(dest_hbm.at[pl.ds(i * rows, rows)], idx_ref, idx_sem)
    idx_copy.start()
    idx_copy.wait()

    _start_row_copies(
        lambda r: pltpu.make_async_copy(src_ref.at[pl.ds(r, 1)], out_hbm.at[pl.ds(idx_ref[r], 1)], sem), rows)
    _row_copy_wait(src_ref, out_hbm, sem, rows)

    @pl.when(i == pl.num_programs(0) - 1)
    def _():
        fill(lambda cp: cp.wait())


def _scatter_rows(src, dest, fill_lo, fill_hi, n_used, n_rows_out):
    n_tok, d = src.shape
    rows = min(PERM_TILE, n_tok)
    kern = functools.partial(_scatter_rows_kernel, rows=rows, n_classes=N_CLASSES,
                             n_blocks=n_rows_out // MOE_TILE)
    grid_spec = pltpu.PrefetchScalarGridSpec(
        num_scalar_prefetch=3,
        grid=(n_tok // rows,),
        in_specs=[pl.BlockSpec(memory_space=pl.ANY),
                  pl.BlockSpec((rows, d), lambda i, *_: (i, 0))],
        out_specs=pl.BlockSpec(memory_space=pl.ANY),
        scratch_shapes=[pltpu.SMEM((rows,), jnp.int32), pltpu.VMEM((MOE_TILE, d), F32),
                        pltpu.SemaphoreType.DMA, pltpu.SemaphoreType.DMA, pltpu.SemaphoreType.DMA],
    )
    return pl.pallas_call(
        kern,
        grid_spec=grid_spec,
        out_shape=jax.ShapeDtypeStruct((n_rows_out, d), F32),
        compiler_params=_cparams("arbitrary"),
        name="moe_scatter_rows",
    )(fill_lo, fill_hi, n_used, dest, src)


def _moe_ffn_kernel(e0_ref, e1_ref, src_ref, nused_ref, x_ref,
                    w1a_ref, w3a_ref, w2a_ref, w1b_ref, w3b_ref, w2b_ref, o_ref,
                    w13_ref, w2_ref, *, d):
    i = pl.program_id(0)
    prev = jnp.maximum(i - 1, 0)
    new_pair = (i == 0) | (e0_ref[i] != e0_ref[prev]) | (e1_ref[i] != e1_ref[prev])

    @pl.when(new_pair & (i < nused_ref[0]))
    def _():
        w13_ref[0] = w1a_ref[...].astype(BF16)
        w13_ref[1] = w3a_ref[...].astype(BF16)
        w13_ref[2] = w1b_ref[...].astype(BF16)
        w13_ref[3] = w3b_ref[...].astype(BF16)
        w2_ref[0] = w2a_ref[...].astype(BF16)
        w2_ref[1] = w2b_ref[...].astype(BF16)

    @pl.when(i < nused_ref[0])
    def _():
        xb = x_ref[:, 0:d].astype(BF16)

        def expert(slot):
            a = _dot(xb, w13_ref[2 * slot])
            b = _dot(xb, w13_ref[2 * slot + 1])
            return _dot((a * jax.nn.sigmoid(a) * b).astype(BF16), w2_ref[slot])

        o_ref[...] = expert(0) * x_ref[:, d:d + 1] + expert(1) * x_ref[:, d + 1:d + 2]

    @pl.when(i >= nused_ref[0])
    def _():
        o_ref[...] = jnp.zeros(o_ref.shape, F32)


def _moe_ffn(sorted_rows, blk_e0, blk_e1, blk_src, n_used, w1, w3, w2):
    n_rows, dp = sorted_rows.shape
    d = dp - GATE_LANES
    f = w1.shape[-1]
    n_blocks = n_rows // MOE_TILE
    wa = lambda i, e0, e1, src, nu: (e0[i], 0, 0)
    wb = lambda i, e0, e1, src, nu: (e1[i], 0, 0)
    grid_spec = pltpu.PrefetchScalarGridSpec(
        num_scalar_prefetch=4,
        grid=(n_blocks,),
        in_specs=[pl.BlockSpec((MOE_TILE, dp), lambda i, e0, e1, src, nu: (src[i], 0)),
                  pl.BlockSpec((None, d, f), wa), pl.BlockSpec((None, d, f), wa), pl.BlockSpec((None, f, d), wa),
                  pl.BlockSpec((None, d, f), wb), pl.BlockSpec((None, d, f), wb), pl.BlockSpec((None, f, d), wb)],
        out_specs=pl.BlockSpec((MOE_TILE, d), lambda i, e0, e1, src, nu: (i, 0)),
        scratch_shapes=[pltpu.VMEM((4, d, f), BF16), pltpu.VMEM((2, f, d), BF16)],
    )
    return pl.pallas_call(
        functools.partial(_moe_ffn_kernel, d=d),
        grid_spec=grid_spec,
        out_shape=jax.ShapeDtypeStruct((n_rows, d), F32),
        compiler_params=_cparams("arbitrary"),
        name="moe_pair_ffn",
    )(blk_e0, blk_e1, blk_src, n_used, sorted_rows, w1, w3, w2, w1, w3, w2)


def _gather_residual_kernel(dest_hbm, x_ref, mod_ref, y_hbm, o_ref, idx_ref, rows_ref, sem, idx_sem, *, rows):
    b = pl.program_id(0)
    i = pl.program_id(1)
    base = (b * pl.num_programs(1) + i) * rows
    idx_copy = pltpu.make_async_copy(dest_hbm.at[pl.ds(base, rows)], idx_ref, idx_sem)
    idx_copy.start()
    idx_copy.wait()

    _start_row_copies(
        lambda r: pltpu.make_async_copy(y_hbm.at[pl.ds(idx_ref[r], 1)], rows_ref.at[pl.ds(r, 1)], sem), rows)
    _row_copy_wait(y_hbm, rows_ref, sem, rows)
    o_ref[...] = x_ref[...] + mod_ref[5:6, :] * rows_ref[...]


def _gather_residual(x, mod, sorted_y, dest):
    bsz, seq, d = x.shape
    rows = min(PERM_TILE, seq)
    kern = functools.partial(_gather_residual_kernel, rows=rows)
    return pl.pallas_call(
        kern,
        grid=(bsz, seq // rows),
        in_specs=[pl.BlockSpec(memory_space=pl.ANY),
                  pl.BlockSpec((None, rows, d), lambda b, i: (b, i, 0)),
                  pl.BlockSpec((None, 6, d), lambda b, i: (b, 0, 0)),
                  pl.BlockSpec(memory_space=pl.ANY)],
        out_specs=pl.BlockSpec((None, rows, d), lambda b, i: (b, i, 0)),
        out_shape=jax.ShapeDtypeStruct((bsz, seq, d), F32),
        scratch_shapes=[pltpu.SMEM((rows,), jnp.int32), pltpu.VMEM((rows, d), F32),
                        pltpu.SemaphoreType.DMA, pltpu.SemaphoreType.DMA],
        compiler_params=_cparams("arbitrary", "arbitrary"),
        name="moe_gather_residual",
    )(dest, x, mod, sorted_y)


def _moe_layer(x, norm_g, mod, router_w, router_b, w1, w3, w2):
    bsz, seq, d = x.shape
    n_tok = bsz * seq
    n_blocks = -(-n_tok // MOE_TILE) + N_CLASSES
    rows, meta, counts = _router(x, norm_g, mod, router_w, router_b)
    dest, blk_e0, blk_e1, blk_src, n_used, fill_lo, fill_hi = _routing_tables(meta, counts, n_blocks)
    sorted_rows = _scatter_rows(rows, dest, fill_lo, fill_hi, n_used, n_blocks * MOE_TILE)
    sorted_y = _moe_ffn(sorted_rows, blk_e0, blk_e1, blk_src, n_used, w1, w3, w2)
    return _gather_residual(x, mod, sorted_y, dest)


def kernel(x, c, mod_w, mod_b, mix_norm_g, ffn_norm_g, ev_in_w, ev_dw_w, ev_dw_b, ev_ln_g, ev_ln_b,
           ev_q_g, ev_k_g, ev_out_w, od_in_w, od_conv_w, od_conv_b, od_rg_w, od_rg_b, od_ig_w, od_ig_b,
           od_lam, od_out_w, router_w, router_b, ex_w1, ex_w3, ex_w2):
    depth = mod_w.shape[0]
    mod = _modulation(c, mod_w, mod_b)
    for layer in range(depth):
        m = mod[layer]
        if layer % 2 == 0:
            e = layer // 2
            conv_ch = ev_dw_w.shape[-1]
            a, q, k, v = _inproj_even(x, mix_norm_g[layer], m, ev_in_w[e], ev_q_g[e], ev_k_g[e], conv_ch)
            u = _conv_module(a, ev_dw_w[e], ev_dw_b[e], ev_ln_g[e], ev_ln_b[e])
            o = _sb_attention(q, k, v, ev_q_g.shape[-1])
            x = _outproj_residual(x, m, [u, o], [ev_out_w[e][:conv_ch], ev_out_w[e][conv_ch:]])
        else:
            o = layer // 2
            proj = _inproj_odd(x, mix_norm_g[layer], m, od_in_w[o])
            mixed = _rglru(proj, od_conv_w[o], od_conv_b[o], od_rg_w[o], od_rg_b[o], od_ig_w[o], od_ig_b[o],
                           od_lam[o])
            x = _outproj_residual(x, m, [mixed], [od_out_w[o]])
        x = _moe_layer(x, ffn_norm_g[layer], m, router_w, router_b, ex_w1[layer], ex_w3[layer], ex_w2[layer])
    return x
```

```python
import functools
import math

import jax
import jax.numpy as jnp
from jax import lax
from jax.experimental import pallas as pl
from jax.experimental.pallas import tpu as pltpu

F32 = jnp.float32
BF16 = jnp.bfloat16
EPS = 1e-6
LOG2E = 1.4426950408889634
LRU_C = 8.0
N_GROUPS = 4
EXPERTS_PER_GROUP = 4
PAIRS_PER_GROUP = 6
N_CLASSES = N_GROUPS * PAIRS_PER_GROUP
_PAIR_LO = (0, 0, 0, 1, 1, 2)
_PAIR_HI = (1, 2, 3, 2, 3, 3)

V7X_VMEM_LIMIT_BYTES = 56 * 1024 * 1024
ROW_TILE = 512
SEQ_TILE = 256
ATT_TILE = 256
MOE_TILE = 512
PERM_TILE = 1024


def _cparams(*sem):
    return pltpu.CompilerParams(dimension_semantics=sem, vmem_limit_bytes=V7X_VMEM_LIMIT_BYTES)


def _dot(a, b):
    return jnp.dot(a, b, preferred_element_type=F32)


def _dot_nt(a, b):
    return lax.dot_general(a, b, (((1,), (1,)), ((), ())), preferred_element_type=F32)


def _split_bf16(x):
    hi = x.astype(BF16)
    lo = (x - hi.astype(F32)).astype(BF16)
    return hi, lo


def _neg_softplus(z):
    return -(jnp.maximum(z, 0.0) + jnp.log1p(jnp.exp(-jnp.abs(z))))


def _norm_mod(x, g, shift, scale):
    ms = jnp.mean(x * x, axis=-1, keepdims=True)
    y = x * lax.rsqrt(ms + EPS) * g
    return y * (1.0 + scale) + shift


def _mod_kernel(c_ref, w_ref, b_ref, o_ref):
    c = c_ref[...]
    cond = (c * jax.nn.sigmoid(c)).astype(BF16)
    o_ref[...] = _dot(cond, w_ref[...].astype(BF16)) + b_ref[...]


def _modulation(c, mod_w, mod_b):
    depth, d, n = mod_w.shape
    bsz = c.shape[0]
    tn = min(n, 1536)
    out = pl.pallas_call(
        _mod_kernel,
        grid=(depth, n // tn),
        in_specs=[pl.BlockSpec((bsz, d), lambda l, j: (0, 0)),
                  pl.BlockSpec((None, d, tn), lambda l, j: (l, 0, j)),
                  pl.BlockSpec((None, 1, tn), lambda l, j: (l, 0, j))],
        out_specs=pl.BlockSpec((None, bsz, tn), lambda l, j: (l, 0, j)),
        out_shape=jax.ShapeDtypeStruct((depth, bsz, n), F32),
        compiler_params=_cparams("arbitrary", "arbitrary"),
        name="adaln_mod",
    )(c, mod_w, mod_b.reshape(depth, 1, n))
    return out.reshape(depth, bsz, 6, d)


def _inproj_even_kernel(x_ref, g_ref, mod_ref, w_ref, e_ref, qg_ref, kg_ref,
                        a_ref, q_ref, k_ref, v_ref, *, conv2, sbw, head_dim):
    h = _norm_mod(x_ref[...], g_ref[...], mod_ref[0:1, :], mod_ref[1:2, :]).astype(BF16)
    a_ref[...] = _dot(h, w_ref[:, 0:conv2])

    def head_rms(t, gain):
        ss = _dot((t * t).astype(BF16), e_ref[...])
        return t * lax.rsqrt(ss * (1.0 / head_dim) + EPS) * gain

    q = _dot(h, w_ref[:, conv2:conv2 + sbw])
    q_ref[...] = (head_rms(q, qg_ref[...]) * (LOG2E / math.sqrt(head_dim))).astype(BF16)
    k = _dot(h, w_ref[:, conv2 + sbw:conv2 + 2 * sbw])
    k_ref[...] = head_rms(k, kg_ref[...]).astype(BF16)
    v_ref[...] = _dot(h, w_ref[:, conv2 + 2 * sbw:conv2 + 3 * sbw]).astype(BF16)


def _inproj_even(x, norm_g, mod, in_w, q_g, k_g, conv_ch):
    bsz, seq, d = x.shape
    n = in_w.shape[1]
    conv2 = 2 * conv_ch
    sbw = (n - conv2) // 3
    head_dim = q_g.shape[0]
    heads = sbw // head_dim
    tm = min(ROW_TILE, seq)
    head_sum = jnp.kron(jnp.eye(heads, dtype=F32), jnp.ones((head_dim, head_dim), F32)).astype(BF16)
    kern = functools.partial(_inproj_even_kernel, conv2=conv2, sbw=sbw, head_dim=head_dim)
    row = lambda b, i: (b, i, 0)
    const = lambda b, i: (0, 0)
    return pl.pallas_call(
        kern,
        grid=(bsz, seq // tm),
        in_specs=[pl.BlockSpec((None, tm, d), row),
                  pl.BlockSpec((1, d), const),
                  pl.BlockSpec((None, 6, d), lambda b, i: (b, 0, 0)),
                  pl.BlockSpec((d, n), const),
                  pl.BlockSpec((sbw, sbw), const),
                  pl.BlockSpec((1, sbw), const),
                  pl.BlockSpec((1, sbw), const)],
        out_specs=[pl.BlockSpec((None, tm, conv2), row),
                   pl.BlockSpec((None, tm, sbw), row),
                   pl.BlockSpec((None, tm, sbw), row),
                   pl.BlockSpec((None, tm, sbw), row)],
        out_shape=[jax.ShapeDtypeStruct((bsz, seq, conv2), F32),
                   jax.ShapeDtypeStruct((bsz, seq, sbw), BF16),
                   jax.ShapeDtypeStruct((bsz, seq, sbw), BF16),
                   jax.ShapeDtypeStruct((bsz, seq, sbw), BF16)],
        compiler_params=_cparams("arbitrary", "arbitrary"),
        name="even_in_proj",
    )(x, norm_g.reshape(1, d), mod, in_w.astype(BF16), head_sum,
      jnp.tile(q_g, heads).reshape(1, sbw), jnp.tile(k_g, heads).reshape(1, sbw))


def _conv_module_kernel(a_ref, w_ref, b_ref, lg_ref, lb_ref, o_ref, buf_ref, sh_ref, *, ch, width, halo, ts, chunk):
    si = pl.program_id(1)
    span = halo + ts - 8

    @pl.when(si == 0)
    def _():
        buf_ref[0:halo, :] = jnp.zeros((halo, ch), F32)

    @pl.when(si > 0)
    def _():
        buf_ref[0:halo, :] = buf_ref[ts:ts + halo, :]

    val = a_ref[:, 0:ch]
    gate = a_ref[:, ch:2 * ch]
    buf_ref[halo:halo + ts, :] = val * jax.nn.sigmoid(gate)

    for p in range(1, 8):
        sh_ref[p, 0:span, :] = buf_ref[p:p + span, :]

    off = halo - (width - 1)
    for c in range(ts // chunk):
        acc = jnp.broadcast_to(b_ref[...], (chunk, ch))
        for k in range(width):
            p = (off + k) % 8
            r0 = c * chunk + off + k - p
            tap = buf_ref[r0:r0 + chunk, :] if p == 0 else sh_ref[p, r0:r0 + chunk, :]
            acc = acc + w_ref[k:k + 1, :] * tap
        mu = jnp.mean(acc, axis=-1, keepdims=True)
        xc = acc - mu
        var = jnp.mean(xc * xc, axis=-1, keepdims=True)
        y = xc * lax.rsqrt(var + EPS) * lg_ref[...] + lb_ref[...]
        o_ref[c * chunk:(c + 1) * chunk, :] = (y * jax.nn.sigmoid(y)).astype(BF16)


def _conv_module(a, dw_w, dw_b, ln_g, ln_b):
    bsz, seq, ch2 = a.shape
    ch = ch2 // 2
    width = dw_w.shape[0]
    halo = -(-(width - 1) // 8) * 8
    ts = min(SEQ_TILE, seq)
    chunk = min(32, ts)
    kern = functools.partial(_conv_module_kernel, ch=ch, width=width, halo=halo, ts=ts, chunk=chunk)
    const = lambda b, i: (0, 0)
    return pl.pallas_call(
        kern,
        grid=(bsz, seq // ts),
        in_specs=[pl.BlockSpec((None, ts, ch2), lambda b, i: (b, i, 0)),
                  pl.BlockSpec((width, ch), const),
                  pl.BlockSpec((1, ch), const),
                  pl.BlockSpec((1, ch), const),
                  pl.BlockSpec((1, ch), const)],
        out_specs=pl.BlockSpec((None, ts, ch), lambda b, i: (b, i, 0)),
        out_shape=jax.ShapeDtypeStruct((bsz, seq, ch), BF16),
        scratch_shapes=[pltpu.VMEM((halo + ts, ch), F32), pltpu.VMEM((8, halo + ts, ch), F32)],
        compiler_params=_cparams("arbitrary", "arbitrary"),
        name="conformer_conv",
    )(a, dw_w, dw_b.reshape(1, ch), ln_g.reshape(1, ch), ln_b.reshape(1, ch))


def _sb_attention_kernel(q_ref, k_ref, v_ref, u_ref, o_ref, *, head_dim, tq):
    qi = pl.program_id(2)
    n_heads = q_ref.shape[1] // head_dim
    head_lanes = [slice(hh * head_dim, (hh + 1) * head_dim) for hh in range(n_heads)]
    rows = lax.broadcasted_iota(jnp.int32, (tq, tq), 0)
    cols = lax.broadcasted_iota(jnp.int32, (tq, tq), 1)
    causal = cols < rows

    def tiles(specs, carry):
        z, sp_b = {}, {}
        order = [(ti, hh) for ti in range(len(specs)) for hh in range(n_heads)]
        for ti, hh in order:
            k0, mask = specs[ti]
            zz = _dot_nt(q_ref[:, head_lanes[hh]], k_ref[pl.ds(k0, tq), head_lanes[hh]])
            ss = jnp.maximum(zz, 0.0) + jnp.log(1.0 + jnp.exp2(-jnp.abs(zz))) * LOG2E
            if mask is not None:
                ss = jnp.where(mask, ss, 0.0)
            z[ti, hh], sp_b[ti, hh] = zz, ss.astype(BF16)
        incl_all = _dot(jnp.concatenate([sp_b[key] for key in order], axis=0), u_ref[...])
        carry = list(carry)
        for idx, (ti, hh) in enumerate(order):
            k0, mask = specs[ti]
            run, acc = carry[2 * hh], carry[2 * hh + 1]
            incl = incl_all[idx * tq:(idx + 1) * tq, :]
            w = jnp.exp2(z[ti, hh] - incl - run)
            if mask is not None:
                w = jnp.where(mask, w, 0.0)
            carry[2 * hh + 1] = acc + _dot(w.astype(BF16), v_ref[pl.ds(k0, tq), head_lanes[hh]])
            carry[2 * hh] = run + incl[:, 0:1]
        return tuple(carry)

    def tile_start(t):
        return pl.multiple_of(t * tq, tq)

    has_prev = jnp.broadcast_to(qi > 0, (tq, tq))
    init = (jnp.zeros((tq, 1), F32), jnp.zeros((tq, head_dim), F32)) * n_heads
    carry = tiles([(tile_start(qi), causal), (tile_start(jnp.maximum(qi - 1, 0)), has_prev)], init)

    def pair_body(step, carry):
        t = qi - 2 - 2 * step
        return tiles([(tile_start(t), None), (tile_start(t - 1), None)], carry)

    def single_body(step, carry):
        return tiles([(0, None)], carry)

    left = jnp.maximum(qi - 1, 0)
    carry = lax.fori_loop(0, lax.shift_right_logical(left, 1), pair_body, carry)
    carry = lax.fori_loop(0, jnp.bitwise_and(left, 1), single_body, carry)
    for hh, lanes in enumerate(head_lanes):
        o_ref[:, lanes] = carry[2 * hh + 1].astype(BF16)


def _sb_attention(q, k, v, head_dim):
    bsz, seq, sbw = q.shape
    tq = min(ATT_TILE, seq)
    lane_blk = min(128, sbw)
    suffix = (lax.broadcasted_iota(jnp.int32, (tq, tq), 0) >= lax.broadcasted_iota(jnp.int32, (tq, tq), 1)).astype(BF16)
    kern = functools.partial(_sb_attention_kernel, head_dim=head_dim, tq=tq)
    return pl.pallas_call(
        kern,
        grid=(bsz, sbw // lane_blk, seq // tq),
        in_specs=[pl.BlockSpec((None, tq, lane_blk), lambda b, h, i: (b, i, h)),
                  pl.BlockSpec((None, seq, lane_blk), lambda b, h, i: (b, 0, h)),
                  pl.BlockSpec((None, seq, lane_blk), lambda b, h, i: (b, 0, h)),
                  pl.BlockSpec((tq, tq), lambda b, h, i: (0, 0))],
        out_specs=pl.BlockSpec((None, tq, lane_blk), lambda b, h, i: (b, i, h)),
        out_shape=jax.ShapeDtypeStruct((bsz, seq, sbw), BF16),
        compiler_params=_cparams("arbitrary", "arbitrary", "arbitrary"),
        name="stickbreak_attn",
    )(q, k, v, suffix)


def _outproj_kernel(*refs, n_in):
    x_ref, mod_ref = refs[0], refs[1]
    ins = refs[2:2 + n_in]
    ws = refs[2 + n_in:2 + 2 * n_in]
    o_ref = refs[2 + 2 * n_in]
    acc = _dot(ins[0][...], ws[0][...])
    for t_ref, w_ref in zip(ins[1:], ws[1:]):
        acc = acc + _dot(t_ref[...], w_ref[...])
    o_ref[...] = x_ref[...] + mod_ref[2:3, :] * acc


def _outproj_residual(x, mod, parts, weights):
    bsz, seq, d = x.shape
    tm = min(ROW_TILE, seq)
    row = lambda b, i: (b, i, 0)
    const = lambda b, i: (0, 0)
    in_specs = [pl.BlockSpec((None, tm, d), row), pl.BlockSpec((None, 6, d), lambda b, i: (b, 0, 0))]
    in_specs += [pl.BlockSpec((None, tm, p.shape[-1]), row) for p in parts]
    in_specs += [pl.BlockSpec(w.shape, const) for w in weights]
    return pl.pallas_call(
        functools.partial(_outproj_kernel, n_in=len(parts)),
        grid=(bsz, seq // tm),
        in_specs=in_specs,
        out_specs=pl.BlockSpec((None, tm, d), row),
        out_shape=jax.ShapeDtypeStruct((bsz, seq, d), F32),
        compiler_params=_cparams("arbitrary", "arbitrary"),
        name="out_proj_residual",
    )(x, mod, *parts, *[w.astype(BF16) for w in weights])


def _inproj_odd_kernel(x_ref, g_ref, mod_ref, w_ref, o_ref, *, n, tn):
    h = _norm_mod(x_ref[...], g_ref[...], mod_ref[0:1, :], mod_ref[1:2, :]).astype(BF16)
    for j in range(n // tn):
        o_ref[:, j * tn:(j + 1) * tn] = _dot(h, w_ref[:, j * tn:(j + 1) * tn])


def _inproj_odd(x, norm_g, mod, in_w):
    bsz, seq, d = x.shape
    n = in_w.shape[1]
    tm = min(ROW_TILE, seq)
    tn = min(512, n)
    row = lambda b, i: (b, i, 0)
    const = lambda b, i: (0, 0)
    return pl.pallas_call(
        functools.partial(_inproj_odd_kernel, n=n, tn=tn),
        grid=(bsz, seq // tm),
        in_specs=[pl.BlockSpec((None, tm, d), row),
                  pl.BlockSpec((1, d), const),
                  pl.BlockSpec((None, 6, d), lambda b, i: (b, 0, 0)),
                  pl.BlockSpec((d, n), const)],
        out_specs=pl.BlockSpec((None, tm, n), row),
        out_shape=jax.ShapeDtypeStruct((bsz, seq, n), F32),
        compiler_params=_cparams("arbitrary", "arbitrary"),
        name="odd_in_proj",
    )(x, norm_g.reshape(1, d), mod, in_w.astype(BF16))


def _rglru_kernel(y_ref, x_ref, cw_ref, cb_ref, gw_ref, rb_ref, ib_ref, lam_ref, o_ref,
                  xbuf_ref, hprev_ref, *, ts, width, bs, nb):
    si = pl.program_id(1)
    halo = 8
    lw = nb * bs

    @pl.when(si == 0)
    def _():
        xbuf_ref[0:halo, :] = jnp.zeros((halo, lw), F32)
        hprev_ref[...] = jnp.zeros((8, lw), F32)

    @pl.when(si > 0)
    def _():
        xbuf_ref[0:halo, :] = xbuf_ref[ts:ts + halo, :]

    xbuf_ref[halo:halo + ts, :] = x_ref[...]
    sub = lax.broadcasted_iota(jnp.int32, (8, bs), 0)
    off = halo - (width - 1)

    for nblk in range(nb):
        lanes = slice(nblk * bs, (nblk + 1) * bs)
        xc = jnp.broadcast_to(cb_ref[:, lanes], (ts, bs))
        for k in range(width):
            xc = xc + cw_ref[k:k + 1, lanes] * xbuf_ref[off + k:off + k + ts, lanes]
        gates = _dot(xc.astype(BF16), gw_ref[nblk])
        r = jax.nn.sigmoid(gates[:, 0:bs] + rb_ref[:, lanes])
        ig = jax.nn.sigmoid(gates[:, bs:2 * bs] + ib_ref[:, lanes])
        log_a = LRU_C * r * _neg_softplus(-lam_ref[:, lanes])
        a = jnp.exp(log_a)
        mult = jnp.sqrt(-jnp.tanh(log_a) * (a * a + 1.0))
        b_in = mult * (ig * xc)

        h_last = hprev_ref[0:1, lanes]
        yv = y_ref[:, lanes]
        for g in range(ts // 8):
            ag = a[g * 8:(g + 1) * 8, :]
            bg = b_in[g * 8:(g + 1) * 8, :]
            for d in (1, 2, 4):
                a_sh = jnp.where(sub >= d, pltpu.roll(ag, d, 0), 1.0)
                b_sh = jnp.where(sub >= d, pltpu.roll(bg, d, 0), 0.0)
                bg = ag * b_sh + bg
                ag = ag * a_sh
            hg = ag * h_last + bg
            h_last = hg[7:8, :]
            yg = yv[g * 8:(g + 1) * 8, :]
            gelu = 0.5 * yg * (1.0 + jnp.tanh(math.sqrt(2.0 / math.pi) * (yg + 0.044715 * (yg * yg * yg))))
            o_ref[g * 8:(g + 1) * 8, lanes] = (gelu * hg).astype(BF16)
        hprev_ref[0:1, lanes] = h_last


def _rglru(proj, conv_w, conv_b, rg_w, rg_b, ig_w, ig_b, lam):
    bsz, seq, n2 = proj.shape
    lw = n2 // 2
    nb, bs, _ = rg_w.shape
    width = conv_w.shape[0]
    ts = min(SEQ_TILE, seq)
    gate_w = jnp.concatenate([rg_w, ig_w], axis=-1).astype(BF16)
    kern = functools.partial(_rglru_kernel, ts=ts, width=width, bs=bs, nb=nb)
    const = lambda b, i: (0, 0)
    return pl.pallas_call(
        kern,
        grid=(bsz, seq // ts),
        in_specs=[pl.BlockSpec((None, ts, lw), lambda b, i: (b, i, 0)),
                  pl.BlockSpec((None, ts, lw), lambda b, i: (b, i, 1)),
                  pl.BlockSpec((width, lw), const),
                  pl.BlockSpec((1, lw), const),
                  pl.BlockSpec((nb, bs, 2 * bs), lambda b, i: (0, 0, 0)),
                  pl.BlockSpec((1, lw), const),
                  pl.BlockSpec((1, lw), const),
                  pl.BlockSpec((1, lw), const)],
        out_specs=pl.BlockSpec((None, ts, lw), lambda b, i: (b, i, 0)),
        out_shape=jax.ShapeDtypeStruct((bsz, seq, lw), BF16),
        scratch_shapes=[pltpu.VMEM((8 + ts, lw), F32), pltpu.VMEM((8, lw), F32)],
        compiler_params=_cparams("arbitrary", "arbitrary"),
        name="rglru",
    )(proj, proj, conv_w, conv_b.reshape(1, lw), gate_w, rg_b.reshape(1, lw), ig_b.reshape(1, lw),
      lam.reshape(1, lw))


CLASS_ROWS = 32
META_ROWS = 8
GATE_LANES = 128


def _first_argmax(vals):
    best, idx = vals[0], jnp.zeros_like(vals[0])
    for j in range(1, len(vals)):
        upd = vals[j] > best
        best = jnp.where(upd, vals[j], best)
        idx = jnp.where(upd, float(j), idx)
    return idx, best


def _pick(idx, vals):
    out = vals[0]
    for j in range(1, len(vals)):
        out = jnp.where(idx == float(j), vals[j], out)
    return out


def _router_kernel(x_ref, g_ref, mod_ref, rwt_ref, rb_ref, tri_ref, h_ref, meta_ref, cnt_ref, base_ref,
                   *, d, tm):
    @pl.when((pl.program_id(0) == 0) & (pl.program_id(1) == 0))
    def _():
        base_ref[...] = jnp.zeros(base_ref.shape, F32)

    h = _norm_mod(x_ref[...], g_ref[...], mod_ref[3:4, :], mod_ref[4:5, :])
    h_ref[:, 0:d] = h
    hi, lo = _split_bf16(h)
    whi, wlo = _split_bf16(rwt_ref[...])
    logits = _dot_nt(whi, hi) + (_dot_nt(whi, lo) + _dot_nt(wlo, hi))
    scores = jax.nn.sigmoid(logits)
    biased = scores + rb_ref[...]
    ne = N_GROUPS * EXPERTS_PER_GROUP
    s_rows = [scores[e:e + 1, :] for e in range(ne)]
    b_rows = [biased[e:e + 1, :] for e in range(ne)]

    group_scores = []
    for g in range(N_GROUPS):
        v = b_rows[g * EXPERTS_PER_GROUP:(g + 1) * EXPERTS_PER_GROUP]
        top2 = v[_PAIR_LO[0]] + v[_PAIR_HI[0]]
        for lo_i, hi_i in zip(_PAIR_LO[1:], _PAIR_HI[1:]):
            top2 = jnp.maximum(top2, v[lo_i] + v[hi_i])
        group_scores.append(top2)
    gidx, _ = _first_argmax(group_scores)
    in_b = [_pick(gidx, [b_rows[g * EXPERTS_PER_GROUP + j] for g in range(N_GROUPS)])
            for j in range(EXPERTS_PER_GROUP)]
    in_s = [_pick(gidx, [s_rows[g * EXPERTS_PER_GROUP + j] for g in range(N_GROUPS)])
            for j in range(EXPERTS_PER_GROUP)]
    i1, _ = _first_argmax(in_b)
    i2, _ = _first_argmax([jnp.where(i1 == float(j), -jnp.inf, in_b[j]) for j in range(EXPERTS_PER_GROUP)])
    sel1 = _pick(i1, in_s)
    sel2 = _pick(i2, in_s)
    total = sel1 + sel2
    gate1 = sel1 / total
    gate2 = sel2 / total
    first_lo = i1 < i2
    e_lo = jnp.minimum(i1, i2)
    e_hi = jnp.maximum(i1, i2)
    pair = jnp.where(e_lo == 0.0, e_hi - 1.0, jnp.where(e_lo == 1.0, e_hi + 1.0, 5.0))
    cls = gidx * float(PAIRS_PER_GROUP) + pair
    gate_lo = jnp.where(first_lo, gate1, gate2)
    gate_hi = jnp.where(first_lo, gate2, gate1)

    class_id = lax.broadcasted_iota(jnp.int32, (CLASS_ROWS, tm), 0).astype(F32)
    onehot = (class_id == cls).astype(F32)
    counted = _dot(onehot.astype(BF16), tri_ref[...])
    rank = jnp.sum(onehot * (counted[:, 0:tm] + base_ref[...]), axis=0, keepdims=True)
    base_ref[...] = base_ref[...] + counted[:, tm:2 * tm]
    cnt_ref[...] = base_ref[:, 0:128]

    zeros = jnp.zeros((1, tm), F32)
    meta_ref[...] = jnp.concatenate([cls, gate_lo, gate_hi, rank] + [zeros] * (META_ROWS - 4), axis=0)
    gate_cols = jnp.concatenate([gate_lo, gate_hi, jnp.zeros((GATE_LANES - 2, tm), F32)], axis=0)
    h_ref[:, d:d + GATE_LANES] = gate_cols.T


def _router(x, norm_g, mod, router_w, router_b):
    bsz, seq, d = x.shape
    ne = router_w.shape[1]
    tm = min(ROW_TILE, seq)
    earlier = (lax.broadcasted_iota(jnp.int32, (tm, tm), 0) < lax.broadcasted_iota(jnp.int32, (tm, tm), 1))
    tri = jnp.concatenate([earlier.astype(BF16), jnp.ones((tm, tm), BF16)], axis=1)
    row = lambda b, i: (b, i, 0)
    const = lambda b, i: (0, 0)
    kern = functools.partial(_router_kernel, d=d, tm=tm)
    rows, meta, counts = pl.pallas_call(
        kern,
        grid=(bsz, seq // tm),
        in_specs=[pl.BlockSpec((None, tm, d), row),
                  pl.BlockSpec((1, d), const),
                  pl.BlockSpec((None, 6, d), lambda b, i: (b, 0, 0)),
                  pl.BlockSpec((ne, d), const),
                  pl.BlockSpec((ne, 1), const),
                  pl.BlockSpec((tm, 2 * tm), const)],
        out_specs=[pl.BlockSpec((None, tm, d + GATE_LANES), row),
                   pl.BlockSpec((None, META_ROWS, tm), lambda b, i: (b, 0, i)),
                   pl.BlockSpec((CLASS_ROWS, 128), const)],
        out_shape=[jax.ShapeDtypeStruct((bsz, seq, d + GATE_LANES), F32),
                   jax.ShapeDtypeStruct((bsz, META_ROWS, seq), F32),
                   jax.ShapeDtypeStruct((CLASS_ROWS, 128), F32)],
        scratch_shapes=[pltpu.VMEM((CLASS_ROWS, tm), F32)],
        compiler_params=_cparams("arbitrary", "arbitrary"),
        name="moe_router",
    )(x, norm_g.reshape(1, d), mod, router_w.T, router_b.reshape(ne, 1).astype(F32), tri)
    return rows.reshape(bsz * seq, d + GATE_LANES), meta, counts


def _routing_tables(meta, counts, n_blocks):
    bsz, _, seq = meta.shape
    cls = meta[:, 0, :].reshape(bsz * seq).astype(jnp.int32)
    rank = meta[:, 3, :].reshape(bsz * seq).astype(jnp.int32)
    counts = counts[:N_CLASSES, 0].astype(jnp.int32)
    padded = (counts + MOE_TILE - 1) // MOE_TILE * MOE_TILE
    pad_end = jnp.cumsum(padded)
    pad_start = pad_end - padded
    class_ids = jnp.arange(N_CLASSES, dtype=jnp.int32)
    dest = rank + jnp.sum(jnp.where(cls[:, None] == class_ids[None, :], pad_start[None, :], 0), axis=1)

    n_used = (pad_end[-1] // MOE_TILE).astype(jnp.int32)
    blk_row = jnp.arange(n_blocks, dtype=jnp.int32) * MOE_TILE
    blk_cls = jnp.minimum(jnp.sum((pad_end[None, :] <= blk_row[:, None]).astype(jnp.int32), axis=1), N_CLASSES - 1)
    blk_grp = blk_cls // PAIRS_PER_GROUP
    blk_pair = blk_cls % PAIRS_PER_GROUP
    pair_ids = jnp.arange(PAIRS_PER_GROUP, dtype=jnp.int32)
    pick = lambda table: jnp.sum(jnp.where(blk_pair[:, None] == pair_ids[None, :],
                                           jnp.asarray(table, jnp.int32)[None, :], 0), axis=1)
    blk_e0 = blk_grp * EXPERTS_PER_GROUP + pick(_PAIR_LO)
    blk_e1 = blk_grp * EXPERTS_PER_GROUP + pick(_PAIR_HI)
    blk_src = jnp.minimum(jnp.arange(n_blocks, dtype=jnp.int32), n_used - 1)
    fill_lo = (pad_start + counts).astype(jnp.int32)
    fill_hi = pad_end.astype(jnp.int32)
    return dest.astype(jnp.int32), blk_e0, blk_e1, blk_src, n_used.reshape(1), fill_lo, fill_hi


ROW_COPY_UNROLL = 8


def _start_row_copies(make_copy, rows):
    def group(g, carry):
        for j in range(ROW_COPY_UNROLL):
            make_copy(g * ROW_COPY_UNROLL + j).start(priority=j % 2)
        return carry

    lax.fori_loop(0, rows // ROW_COPY_UNROLL, group, 0)


def _row_copy_wait(src_ref, dst_ref, sem, rows):
    pltpu.make_async_copy(src_ref.at[pl.ds(0, rows)], dst_ref.at[pl.ds(0, rows)], sem).wait()


def _zero_fill_copies(fill_lo_ref, fill_hi_ref, nused_ref, zero_ref, out_hbm, sem, n_classes, n_blocks, act):
    for c in range(n_classes):
        lo = fill_lo_ref[c]
        hi = fill_hi_ref[c]
        lo8 = jnp.bitwise_and(lo + 7, -8)

        def single(r, carry):
            act(pltpu.make_async_copy(zero_ref.at[pl.ds(0, 1)], out_hbm.at[pl.ds(r, 1)], sem))
            return carry

        lax.fori_loop(lo, lo8, single, 0)
        pos = lo8
        n = hi - lo8
        bit = 8
        while bit < MOE_TILE:
            take = n & bit

            @pl.when(take != 0)
            def _(pos=pos, bit=bit):
                dst = out_hbm.at[pl.ds(pl.multiple_of(pos, 8), bit)]
                act(pltpu.make_async_copy(zero_ref.at[pl.ds(0, bit)], dst, sem))

            pos = pos + take
            bit *= 2

    def tail(blk, carry):
        row0 = pl.multiple_of(blk * MOE_TILE, MOE_TILE)
        act(pltpu.make_async_copy(zero_ref, out_hbm.at[pl.ds(row0, MOE_TILE)], sem))
        return carry

    lax.fori_loop(nused_ref[0], n_blocks, tail, 0)


def _scatter_rows_kernel(fill_lo_ref, fill_hi_ref, nused_ref, dest_hbm, src_ref, out_hbm,
                         idx_ref, ring_ref, zero_ref, sem, idx_sem, fill_sem,
                         *, rows, n_steps, n_classes, n_blocks):
    i = pl.program_id(0)
    slot = jnp.bitwise_and(i, 1)
    fill = functools.partial(_zero_fill_copies, fill_lo_ref, fill_hi_ref, nused_ref, zero_ref, out_hbm,
                             fill_sem, n_classes, n_blocks)

    @pl.when(i == 0)
    def _():
        zero_ref[...] = jnp.zeros(zero_ref.shape, F32)
        fill(lambda cp: cp.start())

    @pl.when(i >= 2)
    def _():
        _row_copy_wait(ring_ref.at[slot], out_hbm, sem.at[slot], rows)

    ring_ref[slot] = src_ref[...]
    idx_copy = pltpu.make_async_copy(dest_hbm.at[pl.ds(i * rows, rows)], idx_ref, idx_sem)
    idx_copy.start()
    idx_copy.wait()
    _start_row_copies(
        lambda r: pltpu.make_async_copy(ring_ref.at[slot, pl.ds(r, 1)], out_hbm.at[pl.ds(idx_ref[r], 1)],
                                        sem.at[slot]), rows)

    @pl.when(i == n_steps - 1)
    def _():
        _row_copy_wait(ring_ref.at[slot], out_hbm, sem.at[slot], rows)
        if n_steps >= 2:
            _row_copy_wait(ring_ref.at[1 - slot], out_hbm, sem.at[1 - slot], rows)
        fill(lambda cp: cp.wait())


def _scatter_rows(src, dest, fill_lo, fill_hi, n_used, n_rows_out):
    n_tok, d = src.shape
    rows = min(PERM_TILE, n_tok)
    n_steps = n_tok // rows
    kern = functools.partial(_scatter_rows_kernel, rows=rows, n_steps=n_steps, n_classes=N_CLASSES,
                             n_blocks=n_rows_out // MOE_TILE)
    grid_spec = pltpu.PrefetchScalarGridSpec(
        num_scalar_prefetch=3,
        grid=(n_steps,),
        in_specs=[pl.BlockSpec(memory_space=pl.ANY),
                  pl.BlockSpec((rows, d), lambda i, *_: (i, 0))],
        out_specs=pl.BlockSpec(memory_space=pl.ANY),
        scratch_shapes=[pltpu.SMEM((rows,), jnp.int32), pltpu.VMEM((2, rows, d), F32),
                        pltpu.VMEM((MOE_TILE, d), F32),
                        pltpu.SemaphoreType.DMA((2,)), pltpu.SemaphoreType.DMA, pltpu.SemaphoreType.DMA],
    )
    return pl.pallas_call(
        kern,
        grid_spec=grid_spec,
        out_shape=jax.ShapeDtypeStruct((n_rows_out, d), F32),
        compiler_params=_cparams("arbitrary"),
        name="moe_scatter_rows",
    )(fill_lo, fill_hi, n_used, dest, src)


def _moe_ffn_kernel(e0_ref, e1_ref, src_ref, nused_ref, x_ref,
                    w1a_ref, w3a_ref, w2a_ref, w1b_ref, w3b_ref, w2b_ref, o_ref,
                    w13_ref, w2_ref, *, d):
    i = pl.program_id(0)
    prev = jnp.maximum(i - 1, 0)
    new_pair = (i == 0) | (e0_ref[i] != e0_ref[prev]) | (e1_ref[i] != e1_ref[prev])

    @pl.when(new_pair & (i < nused_ref[0]))
    def _():
        w13_ref[0] = w1a_ref[...].astype(BF16)
        w13_ref[1] = w3a_ref[...].astype(BF16)
        w13_ref[2] = w1b_ref[...].astype(BF16)
        w13_ref[3] = w3b_ref[...].astype(BF16)
        w2_ref[0] = w2a_ref[...].astype(BF16)
        w2_ref[1] = w2b_ref[...].astype(BF16)

    @pl.when(i < nused_ref[0])
    def _():
        xb = x_ref[:, 0:d].astype(BF16)

        def expert(slot):
            a = _dot(xb, w13_ref[2 * slot])
            b = _dot(xb, w13_ref[2 * slot + 1])
            return _dot((a * jax.nn.sigmoid(a) * b).astype(BF16), w2_ref[slot])

        o_ref[...] = expert(0) * x_ref[:, d:d + 1] + expert(1) * x_ref[:, d + 1:d + 2]

    @pl.when(i >= nused_ref[0])
    def _():
        o_ref[...] = jnp.zeros(o_ref.shape, F32)


def _moe_ffn(sorted_rows, blk_e0, blk_e1, blk_src, n_used, layer, w1, w3, w2):
    n_rows, dp = sorted_rows.shape
    d = dp - GATE_LANES
    f = w1.shape[-1]
    n_blocks = n_rows // MOE_TILE
    wa = lambda i, e0, e1, src, nu: (layer, e0[i], 0, 0)
    wb = lambda i, e0, e1, src, nu: (layer, e1[i], 0, 0)
    grid_spec = pltpu.PrefetchScalarGridSpec(
        num_scalar_prefetch=4,
        grid=(n_blocks,),
        in_specs=[pl.BlockSpec((MOE_TILE, dp), lambda i, e0, e1, src, nu: (src[i], 0)),
                  pl.BlockSpec((None, None, d, f), wa), pl.BlockSpec((None, None, d, f), wa),
                  pl.BlockSpec((None, None, f, d), wa),
                  pl.BlockSpec((None, None, d, f), wb), pl.BlockSpec((None, None, d, f), wb),
                  pl.BlockSpec((None, None, f, d), wb)],
        out_specs=pl.BlockSpec((MOE_TILE, d), lambda i, e0, e1, src, nu: (i, 0)),
        scratch_shapes=[pltpu.VMEM((4, d, f), BF16), pltpu.VMEM((2, f, d), BF16)],
    )
    return pl.pallas_call(
        functools.partial(_moe_ffn_kernel, d=d),
        grid_spec=grid_spec,
        out_shape=jax.ShapeDtypeStruct((n_rows, d), F32),
        compiler_params=_cparams("arbitrary"),
        name="moe_pair_ffn",
    )(blk_e0, blk_e1, blk_src, n_used, sorted_rows, w1, w3, w2, w1, w3, w2)


def _gather_residual_kernel(dest_hbm, x_ref, mod_ref, y_hbm, o_ref, idx_ref, rows_ref, sem, idx_sem,
                            *, rows, n_steps):
    step = pl.program_id(0) * pl.num_programs(1) + pl.program_id(1)

    def fetch(s):
        slot = jnp.bitwise_and(s, 1)
        idx_copy = pltpu.make_async_copy(dest_hbm.at[pl.ds(s * rows, rows)], idx_ref, idx_sem)
        idx_copy.start()
        idx_copy.wait()
        _start_row_copies(
            lambda r: pltpu.make_async_copy(y_hbm.at[pl.ds(idx_ref[r], 1)], rows_ref.at[slot, pl.ds(r, 1)],
                                            sem.at[slot]), rows)

    @pl.when(step == 0)
    def _():
        fetch(step)

    @pl.when(step + 1 < n_steps)
    def _():
        fetch(step + 1)

    slot = jnp.bitwise_and(step, 1)
    _row_copy_wait(y_hbm, rows_ref.at[slot], sem.at[slot], rows)
    o_ref[...] = x_ref[...] + mod_ref[5:6, :] * rows_ref[slot]


def _gather_residual(x, mod, sorted_y, dest):
    bsz, seq, d = x.shape
    rows = min(PERM_TILE, seq)
    kern = functools.partial(_gather_residual_kernel, rows=rows, n_steps=bsz * (seq // rows))
    return pl.pallas_call(
        kern,
        grid=(bsz, seq // rows),
        in_specs=[pl.BlockSpec(memory_space=pl.ANY),
                  pl.BlockSpec((None, rows, d), lambda b, i: (b, i, 0)),
                  pl.BlockSpec((None, 6, d), lambda b, i: (b, 0, 0)),
                  pl.BlockSpec(memory_space=pl.ANY)],
        out_specs=pl.BlockSpec((None, rows, d), lambda b, i: (b, i, 0)),
        out_shape=jax.ShapeDtypeStruct((bsz, seq, d), F32),
        scratch_shapes=[pltpu.SMEM((rows,), jnp.int32), pltpu.VMEM((2, rows, d), F32),
                        pltpu.SemaphoreType.DMA((2,)), pltpu.SemaphoreType.DMA],
        compiler_params=_cparams("arbitrary", "arbitrary"),
        name="moe_gather_residual",
    )(dest, x, mod, sorted_y)


def _moe_layer(x, norm_g, mod, router_w, router_b, layer, w1, w3, w2):
    bsz, seq, d = x.shape
    n_tok = bsz * seq
    n_blocks = -(-n_tok // MOE_TILE) + N_CLASSES
    rows, meta, counts = _router(x, norm_g, mod, router_w, router_b)
    dest, blk_e0, blk_e1, blk_src, n_used, fill_lo, fill_hi = _routing_tables(meta, counts, n_blocks)
    sorted_rows = _scatter_rows(rows, dest, fill_lo, fill_hi, n_used, n_blocks * MOE_TILE)
    sorted_y = _moe_ffn(sorted_rows, blk_e0, blk_e1, blk_src, n_used, layer, w1, w3, w2)
    return _gather_residual(x, mod, sorted_y, dest)


def kernel(x, c, mod_w, mod_b, mix_norm_g, ffn_norm_g, ev_in_w, ev_dw_w, ev_dw_b, ev_ln_g, ev_ln_b,
           ev_q_g, ev_k_g, ev_out_w, od_in_w, od_conv_w, od_conv_b, od_rg_w, od_rg_b, od_ig_w, od_ig_b,
           od_lam, od_out_w, router_w, router_b, ex_w1, ex_w3, ex_w2):
    depth = mod_w.shape[0]
    mod = _modulation(c, mod_w, mod_b)
    for layer in range(depth):
        m = mod[layer]
        if layer % 2 == 0:
            e = layer // 2
            conv_ch = ev_dw_w.shape[-1]
            a, q, k, v = _inproj_even(x, mix_norm_g[layer], m, ev_in_w[e], ev_q_g[e], ev_k_g[e], conv_ch)
            u = _conv_module(a, ev_dw_w[e], ev_dw_b[e], ev_ln_g[e], ev_ln_b[e])
            o = _sb_attention(q, k, v, ev_q_g.shape[-1])
            x = _outproj_residual(x, m, [u, o], [ev_out_w[e][:conv_ch], ev_out_w[e][conv_ch:]])
        else:
            o = layer // 2
            proj = _inproj_odd(x, mix_norm_g[layer], m, od_in_w[o])
            mixed = _rglru(proj, od_conv_w[o], od_conv_b[o], od_rg_w[o], od_rg_b[o], od_ig_w[o], od_ig_b[o],
                           od_lam[o])
            x = _outproj_residual(x, m, [mixed], [od_out_w[o]])
        x = _moe_layer(x, ffn_norm_g[layer], m, router_w, router_b, layer, ex_w1, ex_w3, ex_w2)
    return x
```

```python
import functools
import math

import jax
import jax.numpy as jnp
from jax import lax
from jax.experimental import pallas as pl
from jax.experimental.pallas import tpu as pltpu

F32 = jnp.float32
BF16 = jnp.bfloat16
EPS = 1e-6
LOG2E = 1.4426950408889634
LRU_C = 8.0
N_GROUPS = 4
EXPERTS_PER_GROUP = 4
PAIRS_PER_GROUP = 6
N_CLASSES = N_GROUPS * PAIRS_PER_GROUP
_PAIR_LO = (0, 0, 0, 1, 1, 2)
_PAIR_HI = (1, 2, 3, 2, 3, 3)

V7X_VMEM_LIMIT_BYTES = 56 * 1024 * 1024
ROW_TILE = 512
SEQ_TILE = 256
ATT_TILE = 256
MOE_TILE = 512
PERM_TILE = 1024


def _cparams(*sem):
    return pltpu.CompilerParams(dimension_semantics=sem, vmem_limit_bytes=V7X_VMEM_LIMIT_BYTES)


def _dot(a, b):
    return jnp.dot(a, b, preferred_element_type=F32)


def _dot_nt(a, b):
    return lax.dot_general(a, b, (((1,), (1,)), ((), ())), preferred_element_type=F32)


def _split_bf16(x):
    hi = x.astype(BF16)
    lo = (x - hi.astype(F32)).astype(BF16)
    return hi, lo


def _neg_softplus(z):
    return -(jnp.maximum(z, 0.0) + jnp.log1p(jnp.exp(-jnp.abs(z))))


def _norm_mod(x, g, shift, scale):
    ms = jnp.mean(x * x, axis=-1, keepdims=True)
    y = x * lax.rsqrt(ms + EPS) * g
    return y * (1.0 + scale) + shift


def _mod_kernel(c_ref, w_ref, b_ref, o_ref):
    c = c_ref[...]
    cond = (c * jax.nn.sigmoid(c)).astype(BF16)
    o_ref[...] = _dot(cond, w_ref[...].astype(BF16)) + b_ref[...]


def _modulation(c, mod_w, mod_b):
    depth, d, n = mod_w.shape
    bsz = c.shape[0]
    tn = min(n, 1536)
    out = pl.pallas_call(
        _mod_kernel,
        grid=(depth, n // tn),
        in_specs=[pl.BlockSpec((bsz, d), lambda l, j: (0, 0)),
                  pl.BlockSpec((None, d, tn), lambda l, j: (l, 0, j)),
                  pl.BlockSpec((None, 1, tn), lambda l, j: (l, 0, j))],
        out_specs=pl.BlockSpec((None, bsz, tn), lambda l, j: (l, 0, j)),
        out_shape=jax.ShapeDtypeStruct((depth, bsz, n), F32),
        compiler_params=_cparams("arbitrary", "arbitrary"),
        name="adaln_mod",
    )(c, mod_w, mod_b.reshape(depth, 1, n))
    return out.reshape(depth, bsz, 6, d)


def _inproj_even_kernel(x_ref, g_ref, mod_ref, w_ref, e_ref, qg_ref, kg_ref,
                        a_ref, q_ref, k_ref, v_ref, *, conv2, sbw, head_dim):
    h = _norm_mod(x_ref[...], g_ref[...], mod_ref[0:1, :], mod_ref[1:2, :]).astype(BF16)
    a_ref[...] = _dot(h, w_ref[:, 0:conv2])

    def head_rms(t, gain):
        ss = _dot((t * t).astype(BF16), e_ref[...])
        return t * lax.rsqrt(ss * (1.0 / head_dim) + EPS) * gain

    q = _dot(h, w_ref[:, conv2:conv2 + sbw])
    q_ref[...] = (head_rms(q, qg_ref[...]) * (LOG2E / math.sqrt(head_dim))).astype(BF16)
    k = _dot(h, w_ref[:, conv2 + sbw:conv2 + 2 * sbw])
    k_ref[...] = head_rms(k, kg_ref[...]).astype(BF16)
    v_ref[...] = _dot(h, w_ref[:, conv2 + 2 * sbw:conv2 + 3 * sbw]).astype(BF16)


def _inproj_even(x, norm_g, mod, in_w, q_g, k_g, conv_ch):
    bsz, seq, d = x.shape
    n = in_w.shape[1]
    conv2 = 2 * conv_ch
    sbw = (n - conv2) // 3
    head_dim = q_g.shape[0]
    heads = sbw // head_dim
    tm = min(ROW_TILE, seq)
    head_sum = jnp.kron(jnp.eye(heads, dtype=F32), jnp.ones((head_dim, head_dim), F32)).astype(BF16)
    kern = functools.partial(_inproj_even_kernel, conv2=conv2, sbw=sbw, head_dim=head_dim)
    row = lambda b, i: (b, i, 0)
    const = lambda b, i: (0, 0)
    return pl.pallas_call(
        kern,
        grid=(bsz, seq // tm),
        in_specs=[pl.BlockSpec((None, tm, d), row),
                  pl.BlockSpec((1, d), const),
                  pl.BlockSpec((None, 6, d), lambda b, i: (b, 0, 0)),
                  pl.BlockSpec((d, n), const),
                  pl.BlockSpec((sbw, sbw), const),
                  pl.BlockSpec((1, sbw), const),
                  pl.BlockSpec((1, sbw), const)],
        out_specs=[pl.BlockSpec((None, tm, conv2), row),
                   pl.BlockSpec((None, tm, sbw), row),
                   pl.BlockSpec((None, tm, sbw), row),
                   pl.BlockSpec((None, tm, sbw), row)],
        out_shape=[jax.ShapeDtypeStruct((bsz, seq, conv2), F32),
                   jax.ShapeDtypeStruct((bsz, seq, sbw), BF16),
                   jax.ShapeDtypeStruct((bsz, seq, sbw), BF16),
                   jax.ShapeDtypeStruct((bsz, seq, sbw), BF16)],
        compiler_params=_cparams("arbitrary", "arbitrary"),
        name="even_in_proj",
    )(x, norm_g.reshape(1, d), mod, in_w.astype(BF16), head_sum,
      jnp.tile(q_g, heads).reshape(1, sbw), jnp.tile(k_g, heads).reshape(1, sbw))


def _conv_module_kernel(a_ref, w_ref, b_ref, lg_ref, lb_ref, o_ref, buf_ref, sh_ref, *, ch, width, halo, ts, chunk):
    si = pl.program_id(1)
    span = halo + ts - 8

    @pl.when(si == 0)
    def _():
        buf_ref[0:halo, :] = jnp.zeros((halo, ch), F32)

    @pl.when(si > 0)
    def _():
        buf_ref[0:halo, :] = buf_ref[ts:ts + halo, :]

    val = a_ref[:, 0:ch]
    gate = a_ref[:, ch:2 * ch]
    buf_ref[halo:halo + ts, :] = val * jax.nn.sigmoid(gate)

    for p in range(1, 8):
        sh_ref[p, 0:span, :] = buf_ref[p:p + span, :]

    off = halo - (width - 1)
    for c in range(ts // chunk):
        acc = jnp.broadcast_to(b_ref[...], (chunk, ch))
        for k in range(width):
            p = (off + k) % 8
            r0 = c * chunk + off + k - p
            tap = buf_ref[r0:r0 + chunk, :] if p == 0 else sh_ref[p, r0:r0 + chunk, :]
            acc = acc + w_ref[k:k + 1, :] * tap
        mu = jnp.mean(acc, axis=-1, keepdims=True)
        xc = acc - mu
        var = jnp.mean(xc * xc, axis=-1, keepdims=True)
        y = xc * lax.rsqrt(var + EPS) * lg_ref[...] + lb_ref[...]
        o_ref[c * chunk:(c + 1) * chunk, :] = (y * jax.nn.sigmoid(y)).astype(BF16)


def _conv_module(a, dw_w, dw_b, ln_g, ln_b):
    bsz, seq, ch2 = a.shape
    ch = ch2 // 2
    width = dw_w.shape[0]
    halo = -(-(width - 1) // 8) * 8
    ts = min(SEQ_TILE, seq)
    chunk = min(32, ts)
    kern = functools.partial(_conv_module_kernel, ch=ch, width=width, halo=halo, ts=ts, chunk=chunk)
    const = lambda b, i: (0, 0)
    return pl.pallas_call(
        kern,
        grid=(bsz, seq // ts),
        in_specs=[pl.BlockSpec((None, ts, ch2), lambda b, i: (b, i, 0)),
                  pl.BlockSpec((width, ch), const),
                  pl.BlockSpec((1, ch), const),
                  pl.BlockSpec((1, ch), const),
                  pl.BlockSpec((1, ch), const)],
        out_specs=pl.BlockSpec((None, ts, ch), lambda b, i: (b, i, 0)),
        out_shape=jax.ShapeDtypeStruct((bsz, seq, ch), BF16),
        scratch_shapes=[pltpu.VMEM((halo + ts, ch), F32), pltpu.VMEM((8, halo + ts, ch), F32)],
        compiler_params=_cparams("arbitrary", "arbitrary"),
        name="conformer_conv",
    )(a, dw_w, dw_b.reshape(1, ch), ln_g.reshape(1, ch), ln_b.reshape(1, ch))


def _sb_attention_kernel(q_ref, k_ref, v_ref, u_ref, o_ref, *, head_dim, tq):
    qi = pl.program_id(2)
    n_heads = q_ref.shape[1] // head_dim
    head_lanes = [slice(hh * head_dim, (hh + 1) * head_dim) for hh in range(n_heads)]
    rows = lax.broadcasted_iota(jnp.int32, (tq, tq), 0)
    cols = lax.broadcasted_iota(jnp.int32, (tq, tq), 1)
    causal = cols < rows

    def tiles(specs, carry):
        z, sp_b = {}, {}
        order = [(ti, hh) for ti in range(len(specs)) for hh in range(n_heads)]
        for ti, hh in order:
            k0, mask = specs[ti]
            zz = _dot_nt(q_ref[:, head_lanes[hh]], k_ref[pl.ds(k0, tq), head_lanes[hh]])
            ss = jnp.maximum(zz, 0.0) + jnp.log(1.0 + jnp.exp2(-jnp.abs(zz))) * LOG2E
            if mask is not None:
                ss = jnp.where(mask, ss, 0.0)
            z[ti, hh], sp_b[ti, hh] = zz, ss.astype(BF16)
        incl_all = _dot(jnp.concatenate([sp_b[key] for key in order], axis=0), u_ref[...])
        carry = list(carry)
        for idx, (ti, hh) in enumerate(order):
            k0, mask = specs[ti]
            run, acc = carry[2 * hh], carry[2 * hh + 1]
            incl = incl_all[idx * tq:(idx + 1) * tq, :]
            w = jnp.exp2(z[ti, hh] - incl - run)
            if mask is not None:
                w = jnp.where(mask, w, 0.0)
            carry[2 * hh + 1] = acc + _dot(w.astype(BF16), v_ref[pl.ds(k0, tq), head_lanes[hh]])
            carry[2 * hh] = run + incl[:, 0:1]
        return tuple(carry)

    def tile_start(t):
        return pl.multiple_of(t * tq, tq)

    has_prev = jnp.broadcast_to(qi > 0, (tq, tq))
    init = (jnp.zeros((tq, 1), F32), jnp.zeros((tq, head_dim), F32)) * n_heads
    carry = tiles([(tile_start(qi), causal), (tile_start(jnp.maximum(qi - 1, 0)), has_prev)], init)

    def pair_body(step, carry):
        t = qi - 2 - 2 * step
        return tiles([(tile_start(t), None), (tile_start(t - 1), None)], carry)

    def single_body(step, carry):
        return tiles([(0, None)], carry)

    left = jnp.maximum(qi - 1, 0)
    carry = lax.fori_loop(0, lax.shift_right_logical(left, 1), pair_body, carry)
    carry = lax.fori_loop(0, jnp.bitwise_and(left, 1), single_body, carry)
    for hh, lanes in enumerate(head_lanes):
        o_ref[:, lanes] = carry[2 * hh + 1].astype(BF16)


def _sb_attention(q, k, v, head_dim):
    bsz, seq, sbw = q.shape
    tq = min(ATT_TILE, seq)
    lane_blk = min(128, sbw)
    suffix = (lax.broadcasted_iota(jnp.int32, (tq, tq), 0) >= lax.broadcasted_iota(jnp.int32, (tq, tq), 1)).astype(BF16)
    kern = functools.partial(_sb_attention_kernel, head_dim=head_dim, tq=tq)
    return pl.pallas_call(
        kern,
        grid=(bsz, sbw // lane_blk, seq // tq),
        in_specs=[pl.BlockSpec((None, tq, lane_blk), lambda b, h, i: (b, i, h)),
                  pl.BlockSpec((None, seq, lane_blk), lambda b, h, i: (b, 0, h)),
                  pl.BlockSpec((None, seq, lane_blk), lambda b, h, i: (b, 0, h)),
                  pl.BlockSpec((tq, tq), lambda b, h, i: (0, 0))],
        out_specs=pl.BlockSpec((None, tq, lane_blk), lambda b, h, i: (b, i, h)),
        out_shape=jax.ShapeDtypeStruct((bsz, seq, sbw), BF16),
        compiler_params=_cparams("arbitrary", "arbitrary", "arbitrary"),
        name="stickbreak_attn",
    )(q, k, v, suffix)


def _inproj_odd_kernel(x_ref, g_ref, mod_ref, w_ref, o_ref, *, n, tn):
    h = _norm_mod(x_ref[...], g_ref[...], mod_ref[0:1, :], mod_ref[1:2, :]).astype(BF16)
    for j in range(n // tn):
        o_ref[:, j * tn:(j + 1) * tn] = _dot(h, w_ref[:, j * tn:(j + 1) * tn])


def _inproj_odd(x, norm_g, mod, in_w):
    bsz, seq, d = x.shape
    n = in_w.shape[1]
    tm = min(ROW_TILE, seq)
    tn = min(512, n)
    row = lambda b, i: (b, i, 0)
    const = lambda b, i: (0, 0)
    return pl.pallas_call(
        functools.partial(_inproj_odd_kernel, n=n, tn=tn),
        grid=(bsz, seq // tm),
        in_specs=[pl.BlockSpec((None, tm, d), row),
                  pl.BlockSpec((1, d), const),
                  pl.BlockSpec((None, 6, d), lambda b, i: (b, 0, 0)),
                  pl.BlockSpec((d, n), const)],
        out_specs=pl.BlockSpec((None, tm, n), row),
        out_shape=jax.ShapeDtypeStruct((bsz, seq, n), F32),
        compiler_params=_cparams("arbitrary", "arbitrary"),
        name="odd_in_proj",
    )(x, norm_g.reshape(1, d), mod, in_w.astype(BF16))


def _rglru_kernel(y_ref, x_ref, cw_ref, cb_ref, gw_ref, rb_ref, ib_ref, lam_ref, o_ref,
                  xbuf_ref, hprev_ref, *, ts, width, bs, nb):
    si = pl.program_id(1)
    halo = 8
    lw = nb * bs

    @pl.when(si == 0)
    def _():
        xbuf_ref[0:halo, :] = jnp.zeros((halo, lw), F32)
        hprev_ref[...] = jnp.zeros((8, lw), F32)

    @pl.when(si > 0)
    def _():
        xbuf_ref[0:halo, :] = xbuf_ref[ts:ts + halo, :]

    xbuf_ref[halo:halo + ts, :] = x_ref[...]
    sub = lax.broadcasted_iota(jnp.int32, (8, bs), 0)
    off = halo - (width - 1)

    for nblk in range(nb):
        lanes = slice(nblk * bs, (nblk + 1) * bs)
        xc = jnp.broadcast_to(cb_ref[:, lanes], (ts, bs))
        for k in range(width):
            xc = xc + cw_ref[k:k + 1, lanes] * xbuf_ref[off + k:off + k + ts, lanes]
        gates = _dot(xc.astype(BF16), gw_ref[nblk])
        r = jax.nn.sigmoid(gates[:, 0:bs] + rb_ref[:, lanes])
        ig = jax.nn.sigmoid(gates[:, bs:2 * bs] + ib_ref[:, lanes])
        log_a = LRU_C * r * _neg_softplus(-lam_ref[:, lanes])
        a = jnp.exp(log_a)
        mult = jnp.sqrt(-jnp.tanh(log_a) * (a * a + 1.0))
        b_in = mult * (ig * xc)

        h_last = hprev_ref[0:1, lanes]
        yv = y_ref[:, lanes]
        for g in range(ts // 8):
            ag = a[g * 8:(g + 1) * 8, :]
            bg = b_in[g * 8:(g + 1) * 8, :]
            for d in (1, 2, 4):
                a_sh = jnp.where(sub >= d, pltpu.roll(ag, d, 0), 1.0)
                b_sh = jnp.where(sub >= d, pltpu.roll(bg, d, 0), 0.0)
                bg = ag * b_sh + bg
                ag = ag * a_sh
            hg = ag * h_last + bg
            h_last = hg[7:8, :]
            yg = yv[g * 8:(g + 1) * 8, :]
            gelu = 0.5 * yg * (1.0 + jnp.tanh(math.sqrt(2.0 / math.pi) * (yg + 0.044715 * (yg * yg * yg))))
            o_ref[g * 8:(g + 1) * 8, lanes] = (gelu * hg).astype(BF16)
        hprev_ref[0:1, lanes] = h_last


def _rglru(proj, conv_w, conv_b, rg_w, rg_b, ig_w, ig_b, lam):
    bsz, seq, n2 = proj.shape
    lw = n2 // 2
    nb, bs, _ = rg_w.shape
    width = conv_w.shape[0]
    ts = min(SEQ_TILE, seq)
    gate_w = jnp.concatenate([rg_w, ig_w], axis=-1).astype(BF16)
    kern = functools.partial(_rglru_kernel, ts=ts, width=width, bs=bs, nb=nb)
    const = lambda b, i: (0, 0)
    return pl.pallas_call(
        kern,
        grid=(bsz, seq // ts),
        in_specs=[pl.BlockSpec((None, ts, lw), lambda b, i: (b, i, 0)),
                  pl.BlockSpec((None, ts, lw), lambda b, i: (b, i, 1)),
                  pl.BlockSpec((width, lw), const),
                  pl.BlockSpec((1, lw), const),
                  pl.BlockSpec((nb, bs, 2 * bs), lambda b, i: (0, 0, 0)),
                  pl.BlockSpec((1, lw), const),
                  pl.BlockSpec((1, lw), const),
                  pl.BlockSpec((1, lw), const)],
        out_specs=pl.BlockSpec((None, ts, lw), lambda b, i: (b, i, 0)),
        out_shape=jax.ShapeDtypeStruct((bsz, seq, lw), BF16),
        scratch_shapes=[pltpu.VMEM((8 + ts, lw), F32), pltpu.VMEM((8, lw), F32)],
        compiler_params=_cparams("arbitrary", "arbitrary"),
        name="rglru",
    )(proj, proj, conv_w, conv_b.reshape(1, lw), gate_w, rg_b.reshape(1, lw), ig_b.reshape(1, lw),
      lam.reshape(1, lw))


CLASS_ROWS = 32
META_ROWS = 8
GATE_LANES = 128


def _first_argmax(vals):
    best, idx = vals[0], jnp.zeros_like(vals[0])
    for j in range(1, len(vals)):
        upd = vals[j] > best
        best = jnp.where(upd, vals[j], best)
        idx = jnp.where(upd, float(j), idx)
    return idx, best


def _pick(idx, vals):
    out = vals[0]
    for j in range(1, len(vals)):
        out = jnp.where(idx == float(j), vals[j], out)
    return out


def _route_tile(x, g_ref, mod_ref, rwt_ref, rb_ref, tri_ref, h_ref, meta_ref, cnt_ref, base_ref, *, d, tm):
    @pl.when((pl.program_id(0) == 0) & (pl.program_id(1) == 0))
    def _():
        base_ref[...] = jnp.zeros(base_ref.shape, F32)

    h = _norm_mod(x, g_ref[...], mod_ref[3:4, :], mod_ref[4:5, :])
    h_ref[:, 0:d] = h
    hi, lo = _split_bf16(h)
    whi, wlo = _split_bf16(rwt_ref[...])
    logits = _dot_nt(whi, hi) + (_dot_nt(whi, lo) + _dot_nt(wlo, hi))
    scores = jax.nn.sigmoid(logits)
    biased = scores + rb_ref[...]
    ne = N_GROUPS * EXPERTS_PER_GROUP
    s_rows = [scores[e:e + 1, :] for e in range(ne)]
    b_rows = [biased[e:e + 1, :] for e in range(ne)]

    group_scores = []
    for g in range(N_GROUPS):
        v = b_rows[g * EXPERTS_PER_GROUP:(g + 1) * EXPERTS_PER_GROUP]
        top2 = v[_PAIR_LO[0]] + v[_PAIR_HI[0]]
        for lo_i, hi_i in zip(_PAIR_LO[1:], _PAIR_HI[1:]):
            top2 = jnp.maximum(top2, v[lo_i] + v[hi_i])
        group_scores.append(top2)
    gidx, _ = _first_argmax(group_scores)
    in_b = [_pick(gidx, [b_rows[g * EXPERTS_PER_GROUP + j] for g in range(N_GROUPS)])
            for j in range(EXPERTS_PER_GROUP)]
    in_s = [_pick(gidx, [s_rows[g * EXPERTS_PER_GROUP + j] for g in range(N_GROUPS)])
            for j in range(EXPERTS_PER_GROUP)]
    i1, _ = _first_argmax(in_b)
    i2, _ = _first_argmax([jnp.where(i1 == float(j), -jnp.inf, in_b[j]) for j in range(EXPERTS_PER_GROUP)])
    sel1 = _pick(i1, in_s)
    sel2 = _pick(i2, in_s)
    total = sel1 + sel2
    gate1 = sel1 / total
    gate2 = sel2 / total
    first_lo = i1 < i2
    e_lo = jnp.minimum(i1, i2)
    e_hi = jnp.maximum(i1, i2)
    pair = jnp.where(e_lo == 0.0, e_hi - 1.0, jnp.where(e_lo == 1.0, e_hi + 1.0, 5.0))
    cls = gidx * float(PAIRS_PER_GROUP) + pair
    gate_lo = jnp.where(first_lo, gate1, gate2)
    gate_hi = jnp.where(first_lo, gate2, gate1)

    class_id = lax.broadcasted_iota(jnp.int32, (CLASS_ROWS, tm), 0).astype(F32)
    onehot = (class_id == cls).astype(F32)
    counted = _dot(onehot.astype(BF16), tri_ref[...])
    rank = jnp.sum(onehot * (counted[:, 0:tm] + base_ref[...]), axis=0, keepdims=True)
    base_ref[...] = base_ref[...] + counted[:, tm:2 * tm]
    cnt_ref[...] = base_ref[:, 0:128]

    zeros = jnp.zeros((1, tm), F32)
    meta_ref[...] = jnp.concatenate([cls, gate_lo, gate_hi, rank] + [zeros] * (META_ROWS - 4), axis=0)
    gate_cols = jnp.concatenate([gate_lo, gate_hi, jnp.zeros((GATE_LANES - 2, tm), F32)], axis=0)
    h_ref[:, d:d + GATE_LANES] = gate_cols.T


def _outproj_router_kernel(*refs, n_in, d, tm):
    x_ref, mod_ref = refs[0], refs[1]
    ins = refs[2:2 + n_in]
    ws = refs[2 + n_in:2 + 2 * n_in]
    g_ref, rwt_ref, rb_ref, tri_ref, x1_ref, h_ref, meta_ref, cnt_ref, base_ref = refs[2 + 2 * n_in:]
    acc = _dot(ins[0][...], ws[0][...])
    for t_ref, w_ref in zip(ins[1:], ws[1:]):
        acc = acc + _dot(t_ref[...], w_ref[...])
    x1 = x_ref[...] + mod_ref[2:3, :] * acc
    x1_ref[...] = x1
    _route_tile(x1, g_ref, mod_ref, rwt_ref, rb_ref, tri_ref, h_ref, meta_ref, cnt_ref, base_ref, d=d, tm=tm)


def _outproj_router(x, mod, parts, weights, norm_g, router_w, router_b):
    bsz, seq, d = x.shape
    ne = router_w.shape[1]
    tm = min(ROW_TILE, seq)
    earlier = (lax.broadcasted_iota(jnp.int32, (tm, tm), 0) < lax.broadcasted_iota(jnp.int32, (tm, tm), 1))
    tri = jnp.concatenate([earlier.astype(BF16), jnp.ones((tm, tm), BF16)], axis=1)
    row = lambda b, i: (b, i, 0)
    const = lambda b, i: (0, 0)
    in_specs = [pl.BlockSpec((None, tm, d), row), pl.BlockSpec((None, 6, d), lambda b, i: (b, 0, 0))]
    in_specs += [pl.BlockSpec((None, tm, p.shape[-1]), row) for p in parts]
    in_specs += [pl.BlockSpec(w.shape, const) for w in weights]
    in_specs += [pl.BlockSpec((1, d), const), pl.BlockSpec((ne, d), const), pl.BlockSpec((ne, 1), const),
                 pl.BlockSpec((tm, 2 * tm), const)]
    x1, rows, meta, counts = pl.pallas_call(
        functools.partial(_outproj_router_kernel, n_in=len(parts), d=d, tm=tm),
        grid=(bsz, seq // tm),
        in_specs=in_specs,
        out_specs=[pl.BlockSpec((None, tm, d), row),
                   pl.BlockSpec((None, tm, d + GATE_LANES), row),
                   pl.BlockSpec((None, META_ROWS, tm), lambda b, i: (b, 0, i)),
                   pl.BlockSpec((CLASS_ROWS, 128), const)],
        out_shape=[jax.ShapeDtypeStruct((bsz, seq, d), F32),
                   jax.ShapeDtypeStruct((bsz, seq, d + GATE_LANES), F32),
                   jax.ShapeDtypeStruct((bsz, META_ROWS, seq), F32),
                   jax.ShapeDtypeStruct((CLASS_ROWS, 128), F32)],
        scratch_shapes=[pltpu.VMEM((CLASS_ROWS, tm), F32)],
        compiler_params=_cparams("arbitrary", "arbitrary"),
        name="out_proj_router",
    )(x, mod, *parts, *[w.astype(BF16) for w in weights], norm_g.reshape(1, d), router_w.T,
      router_b.reshape(ne, 1).astype(F32), tri)
    return x1, rows.reshape(bsz * seq, d + GATE_LANES), meta, counts


def _routing_tables(meta, counts, n_blocks):
    bsz, _, seq = meta.shape
    cls = meta[:, 0, :].reshape(bsz * seq).astype(jnp.int32)
    rank = meta[:, 3, :].reshape(bsz * seq).astype(jnp.int32)
    counts = counts[:N_CLASSES, 0].astype(jnp.int32)
    padded = (counts + MOE_TILE - 1) // MOE_TILE * MOE_TILE
    pad_end = jnp.cumsum(padded)
    pad_start = pad_end - padded
    class_ids = jnp.arange(N_CLASSES, dtype=jnp.int32)
    dest = rank + jnp.sum(jnp.where(cls[:, None] == class_ids[None, :], pad_start[None, :], 0), axis=1)

    n_used = (pad_end[-1] // MOE_TILE).astype(jnp.int32)
    blk_row = jnp.arange(n_blocks, dtype=jnp.int32) * MOE_TILE
    blk_cls = jnp.minimum(jnp.sum((pad_end[None, :] <= blk_row[:, None]).astype(jnp.int32), axis=1), N_CLASSES - 1)
    blk_grp = blk_cls // PAIRS_PER_GROUP
    blk_pair = blk_cls % PAIRS_PER_GROUP
    pair_ids = jnp.arange(PAIRS_PER_GROUP, dtype=jnp.int32)
    pick = lambda table: jnp.sum(jnp.where(blk_pair[:, None] == pair_ids[None, :],
                                           jnp.asarray(table, jnp.int32)[None, :], 0), axis=1)
    blk_e0 = blk_grp * EXPERTS_PER_GROUP + pick(_PAIR_LO)
    blk_e1 = blk_grp * EXPERTS_PER_GROUP + pick(_PAIR_HI)
    blk_src = jnp.minimum(jnp.arange(n_blocks, dtype=jnp.int32), n_used - 1)
    fill_lo = (pad_start + counts).astype(jnp.int32)
    fill_hi = pad_end.astype(jnp.int32)
    return dest.astype(jnp.int32), blk_e0, blk_e1, blk_src, n_used.reshape(1), fill_lo, fill_hi


ROW_COPY_UNROLL = 8


def _start_row_copies(make_copy, rows):
    def group(g, carry):
        for j in range(ROW_COPY_UNROLL):
            make_copy(g * ROW_COPY_UNROLL + j).start(priority=j % 2)
        return carry

    lax.fori_loop(0, rows // ROW_COPY_UNROLL, group, 0)


def _row_copy_wait(src_ref, dst_ref, sem, rows):
    pltpu.make_async_copy(src_ref.at[pl.ds(0, rows)], dst_ref.at[pl.ds(0, rows)], sem).wait()


def _zero_fill_copies(fill_lo_ref, fill_hi_ref, nused_ref, zero_ref, out_hbm, sem, n_classes, n_blocks, act):
    for c in range(n_classes):
        lo = fill_lo_ref[c]
        hi = fill_hi_ref[c]
        lo8 = jnp.bitwise_and(lo + 7, -8)

        def single(r, carry):
            act(pltpu.make_async_copy(zero_ref.at[pl.ds(0, 1)], out_hbm.at[pl.ds(r, 1)], sem))
            return carry

        lax.fori_loop(lo, lo8, single, 0)
        pos = lo8
        n = hi - lo8
        bit = 8
        while bit < MOE_TILE:
            take = n & bit

            @pl.when(take != 0)
            def _(pos=pos, bit=bit):
                dst = out_hbm.at[pl.ds(pl.multiple_of(pos, 8), bit)]
                act(pltpu.make_async_copy(zero_ref.at[pl.ds(0, bit)], dst, sem))

            pos = pos + take
            bit *= 2

    def tail(blk, carry):
        row0 = pl.multiple_of(blk * MOE_TILE, MOE_TILE)
        act(pltpu.make_async_copy(zero_ref, out_hbm.at[pl.ds(row0, MOE_TILE)], sem))
        return carry

    lax.fori_loop(nused_ref[0], n_blocks, tail, 0)


def _scatter_rows_kernel(fill_lo_ref, fill_hi_ref, nused_ref, dest_hbm, src_ref, out_hbm,
                         idx_ref, ring_ref, zero_ref, sem, idx_sem, fill_sem,
                         *, rows, n_steps, n_classes, n_blocks):
    i = pl.program_id(0)
    slot = jnp.bitwise_and(i, 1)
    fill = functools.partial(_zero_fill_copies, fill_lo_ref, fill_hi_ref, nused_ref, zero_ref, out_hbm,
                             fill_sem, n_classes, n_blocks)

    @pl.when(i == 0)
    def _():
        zero_ref[...] = jnp.zeros(zero_ref.shape, F32)
        fill(lambda cp: cp.start())

    @pl.when(i >= 2)
    def _():
        _row_copy_wait(ring_ref.at[slot], out_hbm, sem.at[slot], rows)

    ring_ref[slot] = src_ref[...]
    idx_copy = pltpu.make_async_copy(dest_hbm.at[pl.ds(i * rows, rows)], idx_ref, idx_sem)
    idx_copy.start()
    idx_copy.wait()
    _start_row_copies(
        lambda r: pltpu.make_async_copy(ring_ref.at[slot, pl.ds(r, 1)], out_hbm.at[pl.ds(idx_ref[r], 1)],
                                        sem.at[slot]), rows)

    @pl.when(i == n_steps - 1)
    def _():
        _row_copy_wait(ring_ref.at[slot], out_hbm, sem.at[slot], rows)
        if n_steps >= 2:
            _row_copy_wait(ring_ref.at[1 - slot], out_hbm, sem.at[1 - slot], rows)
        fill(lambda cp: cp.wait())


def _scatter_rows(src, dest, fill_lo, fill_hi, n_used, n_rows_out):
    n_tok, d = src.shape
    rows = min(PERM_TILE, n_tok)
    n_steps = n_tok // rows
    kern = functools.partial(_scatter_rows_kernel, rows=rows, n_steps=n_steps, n_classes=N_CLASSES,
                             n_blocks=n_rows_out // MOE_TILE)
    grid_spec = pltpu.PrefetchScalarGridSpec(
        num_scalar_prefetch=3,
        grid=(n_steps,),
        in_specs=[pl.BlockSpec(memory_space=pl.ANY),
                  pl.BlockSpec((rows, d), lambda i, *_: (i, 0))],
        out_specs=pl.BlockSpec(memory_space=pl.ANY),
        scratch_shapes=[pltpu.SMEM((rows,), jnp.int32), pltpu.VMEM((2, rows, d), F32),
                        pltpu.VMEM((MOE_TILE, d), F32),
                        pltpu.SemaphoreType.DMA((2,)), pltpu.SemaphoreType.DMA, pltpu.SemaphoreType.DMA],
    )
    return pl.pallas_call(
        kern,
        grid_spec=grid_spec,
        out_shape=jax.ShapeDtypeStruct((n_rows_out, d), F32),
        compiler_params=_cparams("arbitrary"),
        name="moe_scatter_rows",
    )(fill_lo, fill_hi, n_used, dest, src)


def _moe_ffn_kernel(e0_ref, e1_ref, src_ref, nused_ref, x_ref,
                    w1a_ref, w3a_ref, w2a_ref, w1b_ref, w3b_ref, w2b_ref, o_ref,
                    w13_ref, w2_ref, *, d):
    i = pl.program_id(0)
    prev = jnp.maximum(i - 1, 0)
    new_pair = (i == 0) | (e0_ref[i] != e0_ref[prev]) | (e1_ref[i] != e1_ref[prev])

    @pl.when(new_pair & (i < nused_ref[0]))
    def _():
        w13_ref[0] = w1a_ref[...].astype(BF16)
        w13_ref[1] = w3a_ref[...].astype(BF16)
        w13_ref[2] = w1b_ref[...].astype(BF16)
        w13_ref[3] = w3b_ref[...].astype(BF16)
        w2_ref[0] = w2a_ref[...].astype(BF16)
        w2_ref[1] = w2b_ref[...].astype(BF16)

    @pl.when(i < nused_ref[0])
    def _():
        xb = x_ref[:, 0:d].astype(BF16)

        def expert(slot):
            a = _dot(xb, w13_ref[2 * slot])
            b = _dot(xb, w13_ref[2 * slot + 1])
            return _dot((a * jax.nn.sigmoid(a) * b).astype(BF16), w2_ref[slot])

        o_ref[...] = expert(0) * x_ref[:, d:d + 1] + expert(1) * x_ref[:, d + 1:d + 2]

    @pl.when(i >= nused_ref[0])
    def _():
        o_ref[...] = jnp.zeros(o_ref.shape, F32)


def _moe_ffn(sorted_rows, blk_e0, blk_e1, blk_src, n_used, layer, w1, w3, w2):
    n_rows, dp = sorted_rows.shape
    d = dp - GATE_LANES
    f = w1.shape[-1]
    n_blocks = n_rows // MOE_TILE
    wa = lambda i, e0, e1, src, nu: (layer, e0[i], 0, 0)
    wb = lambda i, e0, e1, src, nu: (layer, e1[i], 0, 0)
    grid_spec = pltpu.PrefetchScalarGridSpec(
        num_scalar_prefetch=4,
        grid=(n_blocks,),
        in_specs=[pl.BlockSpec((MOE_TILE, dp), lambda i, e0, e1, src, nu: (src[i], 0)),
                  pl.BlockSpec((None, None, d, f), wa), pl.BlockSpec((None, None, d, f), wa),
                  pl.BlockSpec((None, None, f, d), wa),
                  pl.BlockSpec((None, None, d, f), wb), pl.BlockSpec((None, None, d, f), wb),
                  pl.BlockSpec((None, None, f, d), wb)],
        out_specs=pl.BlockSpec((MOE_TILE, d), lambda i, e0, e1, src, nu: (i, 0)),
        scratch_shapes=[pltpu.VMEM((4, d, f), BF16), pltpu.VMEM((2, f, d), BF16)],
    )
    return pl.pallas_call(
        functools.partial(_moe_ffn_kernel, d=d),
        grid_spec=grid_spec,
        out_shape=jax.ShapeDtypeStruct((n_rows, d), F32),
        compiler_params=_cparams("arbitrary"),
        name="moe_pair_ffn",
    )(blk_e0, blk_e1, blk_src, n_used, sorted_rows, w1, w3, w2, w1, w3, w2)


def _gather_residual_kernel(dest_hbm, x_ref, mod_ref, y_hbm, o_ref, idx_ref, rows_ref, sem, idx_sem,
                            *, rows, n_steps):
    step = pl.program_id(0) * pl.num_programs(1) + pl.program_id(1)

    def fetch(s):
        slot = jnp.bitwise_and(s, 1)
        idx_copy = pltpu.make_async_copy(dest_hbm.at[pl.ds(s * rows, rows)], idx_ref, idx_sem)
        idx_copy.start()
        idx_copy.wait()
        _start_row_copies(
            lambda r: pltpu.make_async_copy(y_hbm.at[pl.ds(idx_ref[r], 1)], rows_ref.at[slot, pl.ds(r, 1)],
                                            sem.at[slot]), rows)

    @pl.when(step == 0)
    def _():
        fetch(step)

    @pl.when(step + 1 < n_steps)
    def _():
        fetch(step + 1)

    slot = jnp.bitwise_and(step, 1)
    _row_copy_wait(y_hbm, rows_ref.at[slot], sem.at[slot], rows)
    o_ref[...] = x_ref[...] + mod_ref[5:6, :] * rows_ref[slot]


def _gather_residual(x, mod, sorted_y, dest):
    bsz, seq, d = x.shape
    rows = min(PERM_TILE, seq)
    kern = functools.partial(_gather_residual_kernel, rows=rows, n_steps=bsz * (seq // rows))
    return pl.pallas_call(
        kern,
        grid=(bsz, seq // rows),
        in_specs=[pl.BlockSpec(memory_space=pl.ANY),
                  pl.BlockSpec((None, rows, d), lambda b, i: (b, i, 0)),
                  pl.BlockSpec((None, 6, d), lambda b, i: (b, 0, 0)),
                  pl.BlockSpec(memory_space=pl.ANY)],
        out_specs=pl.BlockSpec((None, rows, d), lambda b, i: (b, i, 0)),
        out_shape=jax.ShapeDtypeStruct((bsz, seq, d), F32),
        scratch_shapes=[pltpu.SMEM((rows,), jnp.int32), pltpu.VMEM((2, rows, d), F32),
                        pltpu.SemaphoreType.DMA((2,)), pltpu.SemaphoreType.DMA],
        compiler_params=_cparams("arbitrary", "arbitrary"),
        name="moe_gather_residual",
    )(dest, x, mod, sorted_y)


def _moe_layer(x, rows, meta, counts, mod, layer, w1, w3, w2):
    bsz, seq, d = x.shape
    n_tok = bsz * seq
    n_blocks = -(-n_tok // MOE_TILE) + N_CLASSES
    dest, blk_e0, blk_e1, blk_src, n_used, fill_lo, fill_hi = _routing_tables(meta, counts, n_blocks)
    sorted_rows = _scatter_rows(rows, dest, fill_lo, fill_hi, n_used, n_blocks * MOE_TILE)
    sorted_y = _moe_ffn(sorted_rows, blk_e0, blk_e1, blk_src, n_used, layer, w1, w3, w2)
    return _gather_residual(x, mod, sorted_y, dest)


def kernel(x, c, mod_w, mod_b, mix_norm_g, ffn_norm_g, ev_in_w, ev_dw_w, ev_dw_b, ev_ln_g, ev_ln_b,
           ev_q_g, ev_k_g, ev_out_w, od_in_w, od_conv_w, od_conv_b, od_rg_w, od_rg_b, od_ig_w, od_ig_b,
           od_lam, od_out_w, router_w, router_b, ex_w1, ex_w3, ex_w2):
    depth = mod_w.shape[0]
    mod = _modulation(c, mod_w, mod_b)
    for layer in range(depth):
        m = mod[layer]
        if layer % 2 == 0:
            e = layer // 2
            conv_ch = ev_dw_w.shape[-1]
            a, q, k, v = _inproj_even(x, mix_norm_g[layer], m, ev_in_w[e], ev_q_g[e], ev_k_g[e], conv_ch)
            u = _conv_module(a, ev_dw_w[e], ev_dw_b[e], ev_ln_g[e], ev_ln_b[e])
            o = _sb_attention(q, k, v, ev_q_g.shape[-1])
            parts, weights = [u, o], [ev_out_w[e][:conv_ch], ev_out_w[e][conv_ch:]]
        else:
            o = layer // 2
            proj = _inproj_odd(x, mix_norm_g[layer], m, od_in_w[o])
            mixed = _rglru(proj, od_conv_w[o], od_conv_b[o], od_rg_w[o], od_rg_b[o], od_ig_w[o], od_ig_b[o],
                           od_lam[o])
            parts, weights = [mixed], [od_out_w[o]]
        x, rows, meta, counts = _outproj_router(x, m, parts, weights, ffn_norm_g[layer], router_w, router_b)
        x = _moe_layer(x, rows, meta, counts, m, layer, ex_w1, ex_w3, ex_w2)
    return x
```

```python
import functools
import math

import jax
import jax.numpy as jnp
from jax import lax
from jax.experimental import pallas as pl
from jax.experimental.pallas import tpu as pltpu

F32 = jnp.float32
BF16 = jnp.bfloat16
EPS = 1e-6
LOG2E = 1.4426950408889634
LRU_C = 8.0
N_GROUPS = 4
EXPERTS_PER_GROUP = 4
PAIRS_PER_GROUP = 6
N_CLASSES = N_GROUPS * PAIRS_PER_GROUP
_PAIR_LO = (0, 0, 0, 1, 1, 2)
_PAIR_HI = (1, 2, 3, 2, 3, 3)

V7X_VMEM_LIMIT_BYTES = 56 * 1024 * 1024
ROW_TILE = 512
SEQ_TILE = 256
ATT_TILE = 256
MOE_TILE = 512
PERM_TILE = 1024


def _cparams(*sem):
    return pltpu.CompilerParams(dimension_semantics=sem, vmem_limit_bytes=V7X_VMEM_LIMIT_BYTES)


def _dot(a, b):
    return jnp.dot(a, b, preferred_element_type=F32)


def _dot_nt(a, b):
    return lax.dot_general(a, b, (((1,), (1,)), ((), ())), preferred_element_type=F32)


def _split_bf16(x):
    hi = x.astype(BF16)
    lo = (x - hi.astype(F32)).astype(BF16)
    return hi, lo


def _neg_softplus(z):
    return -(jnp.maximum(z, 0.0) + jnp.log1p(jnp.exp(-jnp.abs(z))))


def _norm_mod(x, g, shift, scale):
    ms = jnp.mean(x * x, axis=-1, keepdims=True)
    y = x * lax.rsqrt(ms + EPS) * g
    return y * (1.0 + scale) + shift


def _mod_kernel(c_ref, w_ref, b_ref, o_ref):
    c = c_ref[...]
    cond = (c * jax.nn.sigmoid(c)).astype(BF16)
    o_ref[...] = _dot(cond, w_ref[...].astype(BF16)) + b_ref[...]


def _modulation(c, mod_w, mod_b):
    depth, d, n = mod_w.shape
    bsz = c.shape[0]
    tn = min(n, 1536)
    out = pl.pallas_call(
        _mod_kernel,
        grid=(depth, n // tn),
        in_specs=[pl.BlockSpec((bsz, d), lambda l, j: (0, 0)),
                  pl.BlockSpec((None, d, tn), lambda l, j: (l, 0, j)),
                  pl.BlockSpec((None, 1, tn), lambda l, j: (l, 0, j))],
        out_specs=pl.BlockSpec((None, bsz, tn), lambda l, j: (l, 0, j)),
        out_shape=jax.ShapeDtypeStruct((depth, bsz, n), F32),
        compiler_params=_cparams("arbitrary", "arbitrary"),
        name="adaln_mod",
    )(c, mod_w, mod_b.reshape(depth, 1, n))
    return out.reshape(depth, bsz, 6, d)


def _inproj_even_kernel(x_ref, g_ref, mod_ref, w_ref, e_ref, qg_ref, kg_ref,
                        a_ref, q_ref, k_ref, v_ref, *, conv2, sbw, head_dim):
    h = _norm_mod(x_ref[...], g_ref[...], mod_ref[0:1, :], mod_ref[1:2, :]).astype(BF16)
    a_ref[...] = _dot(h, w_ref[:, 0:conv2])

    def head_rms(t, gain):
        ss = _dot((t * t).astype(BF16), e_ref[...])
        return t * lax.rsqrt(ss * (1.0 / head_dim) + EPS) * gain

    q = _dot(h, w_ref[:, conv2:conv2 + sbw])
    q_ref[...] = (head_rms(q, qg_ref[...]) * (LOG2E / math.sqrt(head_dim))).astype(BF16)
    k = _dot(h, w_ref[:, conv2 + sbw:conv2 + 2 * sbw])
    k_ref[...] = head_rms(k, kg_ref[...]).astype(BF16)
    v_ref[...] = _dot(h, w_ref[:, conv2 + 2 * sbw:conv2 + 3 * sbw]).astype(BF16)


def _inproj_even(x, norm_g, mod, in_w, q_g, k_g, conv_ch):
    bsz, seq, d = x.shape
    n = in_w.shape[1]
    conv2 = 2 * conv_ch
    sbw = (n - conv2) // 3
    head_dim = q_g.shape[0]
    heads = sbw // head_dim
    tm = min(ROW_TILE, seq)
    head_sum = jnp.kron(jnp.eye(heads, dtype=F32), jnp.ones((head_dim, head_dim), F32)).astype(BF16)
    kern = functools.partial(_inproj_even_kernel, conv2=conv2, sbw=sbw, head_dim=head_dim)
    row = lambda b, i: (b, i, 0)
    const = lambda b, i: (0, 0)
    return pl.pallas_call(
        kern,
        grid=(bsz, seq // tm),
        in_specs=[pl.BlockSpec((None, tm, d), row),
                  pl.BlockSpec((1, d), const),
                  pl.BlockSpec((None, 6, d), lambda b, i: (b, 0, 0)),
                  pl.BlockSpec((d, n), const),
                  pl.BlockSpec((sbw, sbw), const),
                  pl.BlockSpec((1, sbw), const),
                  pl.BlockSpec((1, sbw), const)],
        out_specs=[pl.BlockSpec((None, tm, conv2), row),
                   pl.BlockSpec((None, tm, sbw), row),
                   pl.BlockSpec((None, tm, sbw), row),
                   pl.BlockSpec((None, tm, sbw), row)],
        out_shape=[jax.ShapeDtypeStruct((bsz, seq, conv2), F32),
                   jax.ShapeDtypeStruct((bsz, seq, sbw), BF16),
                   jax.ShapeDtypeStruct((bsz, seq, sbw), BF16),
                   jax.ShapeDtypeStruct((bsz, seq, sbw), BF16)],
        compiler_params=_cparams("arbitrary", "arbitrary"),
        name="even_in_proj",
    )(x, norm_g.reshape(1, d), mod, in_w.astype(BF16), head_sum,
      jnp.tile(q_g, heads).reshape(1, sbw), jnp.tile(k_g, heads).reshape(1, sbw))


def _conv_module_kernel(a_ref, w_ref, b_ref, lg_ref, lb_ref, o_ref, buf_ref, sh_ref, *, ch, width, halo, ts, chunk):
    si = pl.program_id(1)
    span = halo + ts - 8

    @pl.when(si == 0)
    def _():
        buf_ref[0:halo, :] = jnp.zeros((halo, ch), F32)

    @pl.when(si > 0)
    def _():
        buf_ref[0:halo, :] = buf_ref[ts:ts + halo, :]

    val = a_ref[:, 0:ch]
    gate = a_ref[:, ch:2 * ch]
    buf_ref[halo:halo + ts, :] = val * jax.nn.sigmoid(gate)

    for p in range(1, 8):
        sh_ref[p, 0:span, :] = buf_ref[p:p + span, :]

    off = halo - (width - 1)
    for c in range(ts // chunk):
        acc = jnp.broadcast_to(b_ref[...], (chunk, ch))
        for k in range(width):
            p = (off + k) % 8
            r0 = c * chunk + off + k - p
            tap = buf_ref[r0:r0 + chunk, :] if p == 0 else sh_ref[p, r0:r0 + chunk, :]
            acc = acc + w_ref[k:k + 1, :] * tap
        mu = jnp.mean(acc, axis=-1, keepdims=True)
        xc = acc - mu
        var = jnp.mean(xc * xc, axis=-1, keepdims=True)
        y = xc * lax.rsqrt(var + EPS) * lg_ref[...] + lb_ref[...]
        o_ref[c * chunk:(c + 1) * chunk, :] = (y * jax.nn.sigmoid(y)).astype(BF16)


def _conv_module(a, dw_w, dw_b, ln_g, ln_b):
    bsz, seq, ch2 = a.shape
    ch = ch2 // 2
    width = dw_w.shape[0]
    halo = -(-(width - 1) // 8) * 8
    ts = min(SEQ_TILE, seq)
    chunk = min(32, ts)
    kern = functools.partial(_conv_module_kernel, ch=ch, width=width, halo=halo, ts=ts, chunk=chunk)
    const = lambda b, i: (0, 0)
    return pl.pallas_call(
        kern,
        grid=(bsz, seq // ts),
        in_specs=[pl.BlockSpec((None, ts, ch2), lambda b, i: (b, i, 0)),
                  pl.BlockSpec((width, ch), const),
                  pl.BlockSpec((1, ch), const),
                  pl.BlockSpec((1, ch), const),
                  pl.BlockSpec((1, ch), const)],
        out_specs=pl.BlockSpec((None, ts, ch), lambda b, i: (b, i, 0)),
        out_shape=jax.ShapeDtypeStruct((bsz, seq, ch), BF16),
        scratch_shapes=[pltpu.VMEM((halo + ts, ch), F32), pltpu.VMEM((8, halo + ts, ch), F32)],
        compiler_params=_cparams("arbitrary", "arbitrary"),
        name="conformer_conv",
    )(a, dw_w, dw_b.reshape(1, ch), ln_g.reshape(1, ch), ln_b.reshape(1, ch))


def _sb_attention_kernel(q_ref, k_ref, v_ref, u_ref, o_ref, *, head_dim, tq):
    qi = pl.program_id(2)
    n_heads = q_ref.shape[1] // head_dim
    head_lanes = [slice(hh * head_dim, (hh + 1) * head_dim) for hh in range(n_heads)]
    rows = lax.broadcasted_iota(jnp.int32, (tq, tq), 0)
    cols = lax.broadcasted_iota(jnp.int32, (tq, tq), 1)
    causal = cols < rows

    def tiles(specs, carry):
        z, sp_b = {}, {}
        order = [(ti, hh) for ti in range(len(specs)) for hh in range(n_heads)]
        for ti, hh in order:
            k0, mask = specs[ti]
            zz = _dot_nt(q_ref[:, head_lanes[hh]], k_ref[pl.ds(k0, tq), head_lanes[hh]])
            ss = jnp.maximum(zz, 0.0) + jnp.log(1.0 + jnp.exp2(-jnp.abs(zz))) * LOG2E
            if mask is not None:
                ss = jnp.where(mask, ss, 0.0)
            z[ti, hh], sp_b[ti, hh] = zz, ss.astype(BF16)
        incl_all = _dot(jnp.concatenate([sp_b[key] for key in order], axis=0), u_ref[...])
        carry = list(carry)
        for idx, (ti, hh) in enumerate(order):
            k0, mask = specs[ti]
            run, acc = carry[2 * hh], carry[2 * hh + 1]
            incl = incl_all[idx * tq:(idx + 1) * tq, :]
            w = jnp.exp2(z[ti, hh] - incl - run)
            if mask is not None:
                w = jnp.where(mask, w, 0.0)
            carry[2 * hh + 1] = acc + _dot(w.astype(BF16), v_ref[pl.ds(k0, tq), head_lanes[hh]])
            carry[2 * hh] = run + incl[:, 0:1]
        return tuple(carry)

    def tile_start(t):
        return pl.multiple_of(t * tq, tq)

    has_prev = jnp.broadcast_to(qi > 0, (tq, tq))
    init = (jnp.zeros((tq, 1), F32), jnp.zeros((tq, head_dim), F32)) * n_heads
    carry = tiles([(tile_start(qi), causal), (tile_start(jnp.maximum(qi - 1, 0)), has_prev)], init)

    def pair_body(step, carry):
        t = qi - 2 - 2 * step
        return tiles([(tile_start(t), None), (tile_start(t - 1), None)], carry)

    def single_body(step, carry):
        return tiles([(0, None)], carry)

    left = jnp.maximum(qi - 1, 0)
    carry = lax.fori_loop(0, lax.shift_right_logical(left, 1), pair_body, carry)
    carry = lax.fori_loop(0, jnp.bitwise_and(left, 1), single_body, carry)
    for hh, lanes in enumerate(head_lanes):
        o_ref[:, lanes] = carry[2 * hh + 1].astype(BF16)


def _sb_attention(q, k, v, head_dim):
    bsz, seq, sbw = q.shape
    tq = min(ATT_TILE, seq)
    lane_blk = min(256, sbw)
    suffix = (lax.broadcasted_iota(jnp.int32, (tq, tq), 0) >= lax.broadcasted_iota(jnp.int32, (tq, tq), 1)).astype(BF16)
    kern = functools.partial(_sb_attention_kernel, head_dim=head_dim, tq=tq)
    return pl.pallas_call(
        kern,
        grid=(bsz, sbw // lane_blk, seq // tq),
        in_specs=[pl.BlockSpec((None, tq, lane_blk), lambda b, h, i: (b, i, h)),
                  pl.BlockSpec((None, seq, lane_blk), lambda b, h, i: (b, 0, h)),
                  pl.BlockSpec((None, seq, lane_blk), lambda b, h, i: (b, 0, h)),
                  pl.BlockSpec((tq, tq), lambda b, h, i: (0, 0))],
        out_specs=pl.BlockSpec((None, tq, lane_blk), lambda b, h, i: (b, i, h)),
        out_shape=jax.ShapeDtypeStruct((bsz, seq, sbw), BF16),
        compiler_params=_cparams("arbitrary", "arbitrary", "arbitrary"),
        name="stickbreak_attn",
    )(q, k, v, suffix)


def _inproj_odd_kernel(x_ref, g_ref, mod_ref, w_ref, o_ref, *, n, tn):
    h = _norm_mod(x_ref[...], g_ref[...], mod_ref[0:1, :], mod_ref[1:2, :]).astype(BF16)
    for j in range(n // tn):
        o_ref[:, j * tn:(j + 1) * tn] = _dot(h, w_ref[:, j * tn:(j + 1) * tn])


def _inproj_odd(x, norm_g, mod, in_w):
    bsz, seq, d = x.shape
    n = in_w.shape[1]
    tm = min(ROW_TILE, seq)
    tn = min(512, n)
    row = lambda b, i: (b, i, 0)
    const = lambda b, i: (0, 0)
    return pl.pallas_call(
        functools.partial(_inproj_odd_kernel, n=n, tn=tn),
        grid=(bsz, seq // tm),
        in_specs=[pl.BlockSpec((None, tm, d), row),
                  pl.BlockSpec((1, d), const),
                  pl.BlockSpec((None, 6, d), lambda b, i: (b, 0, 0)),
                  pl.BlockSpec((d, n), const)],
        out_specs=pl.BlockSpec((None, tm, n), row),
        out_shape=jax.ShapeDtypeStruct((bsz, seq, n), F32),
        compiler_params=_cparams("arbitrary", "arbitrary"),
        name="odd_in_proj",
    )(x, norm_g.reshape(1, d), mod, in_w.astype(BF16))


def _rglru_kernel(y_ref, x_ref, cw_ref, cb_ref, gw_ref, rb_ref, ib_ref, lam_ref, o_ref,
                  xbuf_ref, hprev_ref, *, ts, width, bs, nb):
    si = pl.program_id(1)
    halo = 8
    lw = nb * bs

    @pl.when(si == 0)
    def _():
        xbuf_ref[0:halo, :] = jnp.zeros((halo, lw), F32)
        hprev_ref[...] = jnp.zeros((8, lw), F32)

    @pl.when(si > 0)
    def _():
        xbuf_ref[0:halo, :] = xbuf_ref[ts:ts + halo, :]

    xbuf_ref[halo:halo + ts, :] = x_ref[...]
    sub = lax.broadcasted_iota(jnp.int32, (8, bs), 0)
    off = halo - (width - 1)

    for nblk in range(nb):
        lanes = slice(nblk * bs, (nblk + 1) * bs)
        xc = jnp.broadcast_to(cb_ref[:, lanes], (ts, bs))
        for k in range(width):
            xc = xc + cw_ref[k:k + 1, lanes] * xbuf_ref[off + k:off + k + ts, lanes]
        gates = _dot(xc.astype(BF16), gw_ref[nblk])
        r = jax.nn.sigmoid(gates[:, 0:bs] + rb_ref[:, lanes])
        ig = jax.nn.sigmoid(gates[:, bs:2 * bs] + ib_ref[:, lanes])
        log_a = LRU_C * r * _neg_softplus(-lam_ref[:, lanes])
        a = jnp.exp(log_a)
        mult = jnp.sqrt(-jnp.tanh(log_a) * (a * a + 1.0))
        b_in = mult * (ig * xc)

        h_last = hprev_ref[0:1, lanes]
        yv = y_ref[:, lanes]
        for g in range(ts // 8):
            ag = a[g * 8:(g + 1) * 8, :]
            bg = b_in[g * 8:(g + 1) * 8, :]
            for d in (1, 2, 4):
                a_sh = jnp.where(sub >= d, pltpu.roll(ag, d, 0), 1.0)
                b_sh = jnp.where(sub >= d, pltpu.roll(bg, d, 0), 0.0)
                bg = ag * b_sh + bg
                ag = ag * a_sh
            hg = ag * h_last + bg
            h_last = hg[7:8, :]
            yg = yv[g * 8:(g + 1) * 8, :]
            gelu = 0.5 * yg * (1.0 + jnp.tanh(math.sqrt(2.0 / math.pi) * (yg + 0.044715 * (yg * yg * yg))))
            o_ref[g * 8:(g + 1) * 8, lanes] = (gelu * hg).astype(BF16)
        hprev_ref[0:1, lanes] = h_last


def _rglru(proj, conv_w, conv_b, rg_w, rg_b, ig_w, ig_b, lam):
    bsz, seq, n2 = proj.shape
    lw = n2 // 2
    nb, bs, _ = rg_w.shape
    width = conv_w.shape[0]
    ts = min(SEQ_TILE, seq)
    gate_w = jnp.concatenate([rg_w, ig_w], axis=-1).astype(BF16)
    kern = functools.partial(_rglru_kernel, ts=ts, width=width, bs=bs, nb=nb)
    const = lambda b, i: (0, 0)
    return pl.pallas_call(
        kern,
        grid=(bsz, seq // ts),
        in_specs=[pl.BlockSpec((None, ts, lw), lambda b, i: (b, i, 0)),
                  pl.BlockSpec((None, ts, lw), lambda b, i: (b, i, 1)),
                  pl.BlockSpec((width, lw), const),
                  pl.BlockSpec((1, lw), const),
                  pl.BlockSpec((nb, bs, 2 * bs), lambda b, i: (0, 0, 0)),
                  pl.BlockSpec((1, lw), const),
                  pl.BlockSpec((1, lw), const),
                  pl.BlockSpec((1, lw), const)],
        out_specs=pl.BlockSpec((None, ts, lw), lambda b, i: (b, i, 0)),
        out_shape=jax.ShapeDtypeStruct((bsz, seq, lw), BF16),
        scratch_shapes=[pltpu.VMEM((8 + ts, lw), F32), pltpu.VMEM((8, lw), F32)],
        compiler_params=_cparams("arbitrary", "arbitrary"),
        name="rglru",
    )(proj, proj, conv_w, conv_b.reshape(1, lw), gate_w, rg_b.reshape(1, lw), ig_b.reshape(1, lw),
      lam.reshape(1, lw))


CLASS_ROWS = 32
META_ROWS = 8
GATE_LANES = 128


def _first_argmax(vals):
    best, idx = vals[0], jnp.zeros_like(vals[0])
    for j in range(1, len(vals)):
        upd = vals[j] > best
        best = jnp.where(upd, vals[j], best)
        idx = jnp.where(upd, float(j), idx)
    return idx, best


def _pick(idx, vals):
    out = vals[0]
    for j in range(1, len(vals)):
        out = jnp.where(idx == float(j), vals[j], out)
    return out


def _route_tile(x, g_ref, mod_ref, rwt_ref, rb_ref, tri_ref, h_ref, meta_ref, cnt_ref, base_ref, *, d, tm):
    @pl.when((pl.program_id(0) == 0) & (pl.program_id(1) == 0))
    def _():
        base_ref[...] = jnp.zeros(base_ref.shape, F32)

    h = _norm_mod(x, g_ref[...], mod_ref[3:4, :], mod_ref[4:5, :])
    h_ref[:, 0:d] = h
    hi, lo = _split_bf16(h)
    whi, wlo = _split_bf16(rwt_ref[...])
    logits = _dot_nt(whi, hi) + (_dot_nt(whi, lo) + _dot_nt(wlo, hi))
    scores = jax.nn.sigmoid(logits)
    biased = scores + rb_ref[...]
    ne = N_GROUPS * EXPERTS_PER_GROUP
    s_rows = [scores[e:e + 1, :] for e in range(ne)]
    b_rows = [biased[e:e + 1, :] for e in range(ne)]

    group_scores = []
    for g in range(N_GROUPS):
        v = b_rows[g * EXPERTS_PER_GROUP:(g + 1) * EXPERTS_PER_GROUP]
        top2 = v[_PAIR_LO[0]] + v[_PAIR_HI[0]]
        for lo_i, hi_i in zip(_PAIR_LO[1:], _PAIR_HI[1:]):
            top2 = jnp.maximum(top2, v[lo_i] + v[hi_i])
        group_scores.append(top2)
    gidx, _ = _first_argmax(group_scores)
    in_b = [_pick(gidx, [b_rows[g * EXPERTS_PER_GROUP + j] for g in range(N_GROUPS)])
            for j in range(EXPERTS_PER_GROUP)]
    in_s = [_pick(gidx, [s_rows[g * EXPERTS_PER_GROUP + j] for g in range(N_GROUPS)])
            for j in range(EXPERTS_PER_GROUP)]
    i1, _ = _first_argmax(in_b)
    i2, _ = _first_argmax([jnp.where(i1 == float(j), -jnp.inf, in_b[j]) for j in range(EXPERTS_PER_GROUP)])
    sel1 = _pick(i1, in_s)
    sel2 = _pick(i2, in_s)
    total = sel1 + sel2
    gate1 = sel1 / total
    gate2 = sel2 / total
    first_lo = i1 < i2
    e_lo = jnp.minimum(i1, i2)
    e_hi = jnp.maximum(i1, i2)
    pair = jnp.where(e_lo == 0.0, e_hi - 1.0, jnp.where(e_lo == 1.0, e_hi + 1.0, 5.0))
    cls = gidx * float(PAIRS_PER_GROUP) + pair
    gate_lo = jnp.where(first_lo, gate1, gate2)
    gate_hi = jnp.where(first_lo, gate2, gate1)

    class_id = lax.broadcasted_iota(jnp.int32, (CLASS_ROWS, tm), 0).astype(F32)
    onehot = (class_id == cls).astype(F32)
    counted = _dot(onehot.astype(BF16), tri_ref[...])
    rank = jnp.sum(onehot * (counted[:, 0:tm] + base_ref[...]), axis=0, keepdims=True)
    base_ref[...] = base_ref[...] + counted[:, tm:2 * tm]
    cnt_ref[...] = base_ref[:, 0:128]

    zeros = jnp.zeros((1, tm), F32)
    meta_ref[...] = jnp.concatenate([cls, gate_lo, gate_hi, rank] + [zeros] * (META_ROWS - 4), axis=0)
    gate_cols = jnp.concatenate([gate_lo, gate_hi, jnp.zeros((GATE_LANES - 2, tm), F32)], axis=0)
    h_ref[:, d:d + GATE_LANES] = gate_cols.T


def _outproj_router_kernel(*refs, n_in, d, tm):
    x_ref, mod_ref = refs[0], refs[1]
    ins = refs[2:2 + n_in]
    ws = refs[2 + n_in:2 + 2 * n_in]
    g_ref, rwt_ref, rb_ref, tri_ref, x1_ref, h_ref, meta_ref, cnt_ref, base_ref = refs[2 + 2 * n_in:]
    acc = _dot(ins[0][...], ws[0][...])
    for t_ref, w_ref in zip(ins[1:], ws[1:]):
        acc = acc + _dot(t_ref[...], w_ref[...])
    x1 = x_ref[...] + mod_ref[2:3, :] * acc
    x1_ref[...] = x1
    _route_tile(x1, g_ref, mod_ref, rwt_ref, rb_ref, tri_ref, h_ref, meta_ref, cnt_ref, base_ref, d=d, tm=tm)


def _outproj_router(x, mod, parts, weights, norm_g, router_w, router_b):
    bsz, seq, d = x.shape
    ne = router_w.shape[1]
    tm = min(ROW_TILE, seq)
    earlier = (lax.broadcasted_iota(jnp.int32, (tm, tm), 0) < lax.broadcasted_iota(jnp.int32, (tm, tm), 1))
    tri = jnp.concatenate([earlier.astype(BF16), jnp.ones((tm, tm), BF16)], axis=1)
    row = lambda b, i: (b, i, 0)
    const = lambda b, i: (0, 0)
    in_specs = [pl.BlockSpec((None, tm, d), row), pl.BlockSpec((None, 6, d), lambda b, i: (b, 0, 0))]
    in_specs += [pl.BlockSpec((None, tm, p.shape[-1]), row) for p in parts]
    in_specs += [pl.BlockSpec(w.shape, const) for w in weights]
    in_specs += [pl.BlockSpec((1, d), const), pl.BlockSpec((ne, d), const), pl.BlockSpec((ne, 1), const),
                 pl.BlockSpec((tm, 2 * tm), const)]
    x1, rows, meta, counts = pl.pallas_call(
        functools.partial(_outproj_router_kernel, n_in=len(parts), d=d, tm=tm),
        grid=(bsz, seq // tm),
        in_specs=in_specs,
        out_specs=[pl.BlockSpec((None, tm, d), row),
                   pl.BlockSpec((None, tm, d + GATE_LANES), row),
                   pl.BlockSpec((None, META_ROWS, tm), lambda b, i: (b, 0, i)),
                   pl.BlockSpec((CLASS_ROWS, 128), const)],
        out_shape=[jax.ShapeDtypeStruct((bsz, seq, d), F32),
                   jax.ShapeDtypeStruct((bsz, seq, d + GATE_LANES), F32),
                   jax.ShapeDtypeStruct((bsz, META_ROWS, seq), F32),
                   jax.ShapeDtypeStruct((CLASS_ROWS, 128), F32)],
        scratch_shapes=[pltpu.VMEM((CLASS_ROWS, tm), F32)],
        compiler_params=_cparams("arbitrary", "arbitrary"),
        name="out_proj_router",
    )(x, mod, *parts, *[w.astype(BF16) for w in weights], norm_g.reshape(1, d), router_w.T,
      router_b.reshape(ne, 1).astype(F32), tri)
    return x1, rows.reshape(bsz * seq, d + GATE_LANES), meta, counts


def _routing_tables(meta, counts, n_blocks):
    bsz, _, seq = meta.shape
    cls = meta[:, 0, :].reshape(bsz * seq).astype(jnp.int32)
    rank = meta[:, 3, :].reshape(bsz * seq).astype(jnp.int32)
    counts = counts[:N_CLASSES, 0].astype(jnp.int32)
    padded = (counts + MOE_TILE - 1) // MOE_TILE * MOE_TILE
    pad_end = jnp.cumsum(padded)
    pad_start = pad_end - padded
    class_ids = jnp.arange(N_CLASSES, dtype=jnp.int32)
    dest = rank + jnp.sum(jnp.where(cls[:, None] == class_ids[None, :], pad_start[None, :], 0), axis=1)

    n_used = (pad_end[-1] // MOE_TILE).astype(jnp.int32)
    blk_row = jnp.arange(n_blocks, dtype=jnp.int32) * MOE_TILE
    blk_cls = jnp.minimum(jnp.sum((pad_end[None, :] <= blk_row[:, None]).astype(jnp.int32), axis=1), N_CLASSES - 1)
    blk_grp = blk_cls // PAIRS_PER_GROUP
    blk_pair = blk_cls % PAIRS_PER_GROUP
    pair_ids = jnp.arange(PAIRS_PER_GROUP, dtype=jnp.int32)
    pick = lambda table: jnp.sum(jnp.where(blk_pair[:, None] == pair_ids[None, :],
                                           jnp.asarray(table, jnp.int32)[None, :], 0), axis=1)
    blk_e0 = blk_grp * EXPERTS_PER_GROUP + pick(_PAIR_LO)
    blk_e1 = blk_grp * EXPERTS_PER_GROUP + pick(_PAIR_HI)
    blk_src = jnp.minimum(jnp.arange(n_blocks, dtype=jnp.int32), n_used - 1)
    fill_lo = (pad_start + counts).astype(jnp.int32)
    fill_hi = pad_end.astype(jnp.int32)
    return dest.astype(jnp.int32), blk_e0, blk_e1, blk_src, n_used.reshape(1), fill_lo, fill_hi


ROW_COPY_UNROLL = 8


def _start_row_copies(make_copy, rows):
    def group(g, carry):
        for j in range(ROW_COPY_UNROLL):
            make_copy(g * ROW_COPY_UNROLL + j).start(priority=j % 2)
        return carry

    lax.fori_loop(0, rows // ROW_COPY_UNROLL, group, 0)


def _row_copy_wait(src_ref, dst_ref, sem, rows):
    pltpu.make_async_copy(src_ref.at[pl.ds(0, rows)], dst_ref.at[pl.ds(0, rows)], sem).wait()


def _zero_fill_copies(fill_lo_ref, fill_hi_ref, nused_ref, zero_ref, out_hbm, sem, n_classes, n_blocks, act):
    for c in range(n_classes):
        lo = fill_lo_ref[c]
        hi = fill_hi_ref[c]
        lo8 = jnp.bitwise_and(lo + 7, -8)

        def single(r, carry):
            act(pltpu.make_async_copy(zero_ref.at[pl.ds(0, 1)], out_hbm.at[pl.ds(r, 1)], sem))
            return carry

        lax.fori_loop(lo, lo8, single, 0)
        pos = lo8
        n = hi - lo8
        bit = 8
        while bit < MOE_TILE:
            take = n & bit

            @pl.when(take != 0)
            def _(pos=pos, bit=bit):
                dst = out_hbm.at[pl.ds(pl.multiple_of(pos, 8), bit)]
                act(pltpu.make_async_copy(zero_ref.at[pl.ds(0, bit)], dst, sem))

            pos = pos + take
            bit *= 2

    def tail(blk, carry):
        row0 = pl.multiple_of(blk * MOE_TILE, MOE_TILE)
        act(pltpu.make_async_copy(zero_ref, out_hbm.at[pl.ds(row0, MOE_TILE)], sem))
        return carry

    lax.fori_loop(nused_ref[0], n_blocks, tail, 0)


def _scatter_rows_kernel(fill_lo_ref, fill_hi_ref, nused_ref, dest_hbm, src_ref, out_hbm,
                         idx_ref, ring_ref, zero_ref, sem, idx_sem, fill_sem,
                         *, rows, n_steps, n_classes, n_blocks):
    i = pl.program_id(0)
    slot = jnp.bitwise_and(i, 1)
    fill = functools.partial(_zero_fill_copies, fill_lo_ref, fill_hi_ref, nused_ref, zero_ref, out_hbm,
                             fill_sem, n_classes, n_blocks)

    @pl.when(i == 0)
    def _():
        zero_ref[...] = jnp.zeros(zero_ref.shape, F32)
        fill(lambda cp: cp.start())

    @pl.when(i >= 2)
    def _():
        _row_copy_wait(ring_ref.at[slot], out_hbm, sem.at[slot], rows)

    ring_ref[slot] = src_ref[...]
    idx_copy = pltpu.make_async_copy(dest_hbm.at[pl.ds(i * rows, rows)], idx_ref, idx_sem)
    idx_copy.start()
    idx_copy.wait()
    _start_row_copies(
        lambda r: pltpu.make_async_copy(ring_ref.at[slot, pl.ds(r, 1)], out_hbm.at[pl.ds(idx_ref[r], 1)],
                                        sem.at[slot]), rows)

    @pl.when(i == n_steps - 1)
    def _():
        _row_copy_wait(ring_ref.at[slot], out_hbm, sem.at[slot], rows)
        if n_steps >= 2:
            _row_copy_wait(ring_ref.at[1 - slot], out_hbm, sem.at[1 - slot], rows)
        fill(lambda cp: cp.wait())


def _scatter_rows(src, dest, fill_lo, fill_hi, n_used, n_rows_out):
    n_tok, d = src.shape
    rows = min(PERM_TILE, n_tok)
    n_steps = n_tok // rows
    kern = functools.partial(_scatter_rows_kernel, rows=rows, n_steps=n_steps, n_classes=N_CLASSES,
                             n_blocks=n_rows_out // MOE_TILE)
    grid_spec = pltpu.PrefetchScalarGridSpec(
        num_scalar_prefetch=3,
        grid=(n_steps,),
        in_specs=[pl.BlockSpec(memory_space=pl.ANY),
                  pl.BlockSpec((rows, d), lambda i, *_: (i, 0))],
        out_specs=pl.BlockSpec(memory_space=pl.ANY),
        scratch_shapes=[pltpu.SMEM((rows,), jnp.int32), pltpu.VMEM((2, rows, d), F32),
                        pltpu.VMEM((MOE_TILE, d), F32),
                        pltpu.SemaphoreType.DMA((2,)), pltpu.SemaphoreType.DMA, pltpu.SemaphoreType.DMA],
    )
    return pl.pallas_call(
        kern,
        grid_spec=grid_spec,
        out_shape=jax.ShapeDtypeStruct((n_rows_out, d), F32),
        compiler_params=_cparams("arbitrary"),
        name="moe_scatter_rows",
    )(fill_lo, fill_hi, n_used, dest, src)


def _moe_ffn_kernel(e0_ref, e1_ref, src_ref, nused_ref, x_ref,
                    w1a_ref, w3a_ref, w2a_ref, w1b_ref, w3b_ref, w2b_ref, o_ref,
                    w13_ref, w2_ref, *, d):
    i = pl.program_id(0)
    prev = jnp.maximum(i - 1, 0)
    new_pair = (i == 0) | (e0_ref[i] != e0_ref[prev]) | (e1_ref[i] != e1_ref[prev])

    @pl.when(new_pair & (i < nused_ref[0]))
    def _():
        w13_ref[0] = w1a_ref[...].astype(BF16)
        w13_ref[1] = w3a_ref[...].astype(BF16)
        w13_ref[2] = w1b_ref[...].astype(BF16)
        w13_ref[3] = w3b_ref[...].astype(BF16)
        w2_ref[0] = w2a_ref[...].astype(BF16)
        w2_ref[1] = w2b_ref[...].astype(BF16)

    @pl.when(i < nused_ref[0])
    def _():
        xb = x_ref[:, 0:d].astype(BF16)

        def expert(slot):
            a = _dot(xb, w13_ref[2 * slot])
            b = _dot(xb, w13_ref[2 * slot + 1])
            return _dot((a * jax.nn.sigmoid(a) * b).astype(BF16), w2_ref[slot])

        o_ref[...] = expert(0) * x_ref[:, d:d + 1] + expert(1) * x_ref[:, d + 1:d + 2]

    @pl.when(i >= nused_ref[0])
    def _():
        o_ref[...] = jnp.zeros(o_ref.shape, F32)


def _moe_ffn(sorted_rows, blk_e0, blk_e1, blk_src, n_used, layer, w1, w3, w2):
    n_rows, dp = sorted_rows.shape
    d = dp - GATE_LANES
    f = w1.shape[-1]
    n_blocks = n_rows // MOE_TILE
    wa = lambda i, e0, e1, src, nu: (layer, e0[i], 0, 0)
    wb = lambda i, e0, e1, src, nu: (layer, e1[i], 0, 0)
    grid_spec = pltpu.PrefetchScalarGridSpec(
        num_scalar_prefetch=4,
        grid=(n_blocks,),
        in_specs=[pl.BlockSpec((MOE_TILE, dp), lambda i, e0, e1, src, nu: (src[i], 0)),
                  pl.BlockSpec((None, None, d, f), wa), pl.BlockSpec((None, None, d, f), wa),
                  pl.BlockSpec((None, None, f, d), wa),
                  pl.BlockSpec((None, None, d, f), wb), pl.BlockSpec((None, None, d, f), wb),
                  pl.BlockSpec((None, None, f, d), wb)],
        out_specs=pl.BlockSpec((MOE_TILE, d), lambda i, e0, e1, src, nu: (i, 0)),
        scratch_shapes=[pltpu.VMEM((4, d, f), BF16), pltpu.VMEM((2, f, d), BF16)],
    )
    return pl.pallas_call(
        functools.partial(_moe_ffn_kernel, d=d),
        grid_spec=grid_spec,
        out_shape=jax.ShapeDtypeStruct((n_rows, d), F32),
        compiler_params=_cparams("arbitrary"),
        name="moe_pair_ffn",
    )(blk_e0, blk_e1, blk_src, n_used, sorted_rows, w1, w3, w2, w1, w3, w2)


def _gather_residual_kernel(dest_hbm, x_ref, mod_ref, y_hbm, o_ref, idx_ref, rows_ref, sem, idx_sem,
                            *, rows, n_steps):
    step = pl.program_id(0) * pl.num_programs(1) + pl.program_id(1)

    def fetch(s):
        slot = jnp.bitwise_and(s, 1)
        idx_copy = pltpu.make_async_copy(dest_hbm.at[pl.ds(s * rows, rows)], idx_ref, idx_sem)
        idx_copy.start()
        idx_copy.wait()
        _start_row_copies(
            lambda r: pltpu.make_async_copy(y_hbm.at[pl.ds(idx_ref[r], 1)], rows_ref.at[slot, pl.ds(r, 1)],
                                            sem.at[slot]), rows)

    @pl.when(step == 0)
    def _():
        fetch(step)

    @pl.when(step + 1 < n_steps)
    def _():
        fetch(step + 1)

    slot = jnp.bitwise_and(step, 1)
    _row_copy_wait(y_hbm, rows_ref.at[slot], sem.at[slot], rows)
    o_ref[...] = x_ref[...] + mod_ref[5:6, :] * rows_ref[slot]


def _gather_residual(x, mod, sorted_y, dest):
    bsz, seq, d = x.shape
    rows = min(PERM_TILE, seq)
    kern = functools.partial(_gather_residual_kernel, rows=rows, n_steps=bsz * (seq // rows))
    return pl.pallas_call(
        kern,
        grid=(bsz, seq // rows),
        in_specs=[pl.BlockSpec(memory_space=pl.ANY),
                  pl.BlockSpec((None, rows, d), lambda b, i: (b, i, 0)),
                  pl.BlockSpec((None, 6, d), lambda b, i: (b, 0, 0)),
                  pl.BlockSpec(memory_space=pl.ANY)],
        out_specs=pl.BlockSpec((None, rows, d), lambda b, i: (b, i, 0)),
        out_shape=jax.ShapeDtypeStruct((bsz, seq, d), F32),
        scratch_shapes=[pltpu.SMEM((rows,), jnp.int32), pltpu.VMEM((2, rows, d), F32),
                        pltpu.SemaphoreType.DMA((2,)), pltpu.SemaphoreType.DMA],
        compiler_params=_cparams("arbitrary", "arbitrary"),
        name="moe_gather_residual",
    )(dest, x, mod, sorted_y)


def _moe_layer(x, rows, meta, counts, mod, layer, w1, w3, w2):
    bsz, seq, d = x.shape
    n_tok = bsz * seq
    n_blocks = -(-n_tok // MOE_TILE) + N_CLASSES
    dest, blk_e0, blk_e1, blk_src, n_used, fill_lo, fill_hi = _routing_tables(meta, counts, n_blocks)
    sorted_rows = _scatter_rows(rows, dest, fill_lo, fill_hi, n_used, n_blocks * MOE_TILE)
    sorted_y = _moe_ffn(sorted_rows, blk_e0, blk_e1, blk_src, n_used, layer, w1, w3, w2)
    return _gather_residual(x, mod, sorted_y, dest)


def kernel(x, c, mod_w, mod_b, mix_norm_g, ffn_norm_g, ev_in_w, ev_dw_w, ev_dw_b, ev_ln_g, ev_ln_b,
           ev_q_g, ev_k_g, ev_out_w, od_in_w, od_conv_w, od_conv_b, od_rg_w, od_rg_b, od_ig_w, od_ig_b,
           od_lam, od_out_w, router_w, router_b, ex_w1, ex_w3, ex_w2):
    depth = mod_w.shape[0]
    mod = _modulation(c, mod_w, mod_b)
    for layer in range(depth):
        m = mod[layer]
        if layer % 2 == 0:
            e = layer // 2
            conv_ch = ev_dw_w.shape[-1]
            a, q, k, v = _inproj_even(x, mix_norm_g[layer], m, ev_in_w[e], ev_q_g[e], ev_k_g[e], conv_ch)
            u = _conv_module(a, ev_dw_w[e], ev_dw_b[e], ev_ln_g[e], ev_ln_b[e])
            o = _sb_attention(q, k, v, ev_q_g.shape[-1])
            parts, weights = [u, o], [ev_out_w[e][:conv_ch], ev_out_w[e][conv_ch:]]
        else:
            o = layer // 2
            proj = _inproj_odd(x, mix_norm_g[layer], m, od_in_w[o])
            mixed = _rglru(proj, od_conv_w[o], od_conv_b[o], od_rg_w[o], od_rg_b[o], od_ig_w[o], od_ig_b[o],
                           od_lam[o])
            parts, weights = [mixed], [od_out_w[o]]
        x, rows, meta, counts = _outproj_router(x, m, parts, weights, ffn_norm_g[layer], router_w, router_b)
        x = _moe_layer(x, rows, meta, counts, m, layer, ex_w1, ex_w3, ex_w2)
    return x
```
